```python
import math
import jax, jax.numpy as jnp
from jax import lax
import numpy as np

D_MODEL = 4096
BATCH = 1
SEQ = 8192
DEPTH = 4

HEAD_DIM = 128
POOL_WINDOWS = (2, 4, 8, 16)
POOL_GROUPS = len(POOL_WINDOWS)
POOL_WIDTH = D_MODEL // 4
POOL_GROUP_DIM = POOL_WIDTH // POOL_GROUPS
DSA_WIDTH = 3 * D_MODEL // 8
DSA_HEADS = DSA_WIDTH // HEAD_DIM
SB_WIDTH = D_MODEL - POOL_WIDTH - DSA_WIDTH
SB_HEADS = SB_WIDTH // HEAD_DIM
MIX_WIDTH = POOL_WIDTH + DSA_WIDTH + SB_WIDTH
IDX_HEADS = 16
IDX_DIM = 64
INDEX_TOPK = 256
D_FF = 3 * D_MODEL // 2
N_BUCKETS = 32
MAX_DISTANCE = 128
Q_BLOCK = 128
EPS = 1e-6
IN_SIZES = (POOL_WIDTH, DSA_WIDTH, DSA_WIDTH, DSA_WIDTH,
            IDX_HEADS * IDX_DIM, IDX_DIM, IDX_HEADS,
            SB_WIDTH, SB_WIDTH, SB_WIDTH)
IN_WIDTH = sum(IN_SIZES)

kernel_name = "hymba_style_pool_dsa_stickbreak_macaron"


def rmsnorm(x, g):
    xf = x.astype(jnp.float32)
    y = xf * lax.rsqrt(jnp.mean(xf * xf, axis=-1, keepdims=True) + EPS)
    return (y * g.astype(jnp.float32)).astype(x.dtype)


def swiglu(x, wg, wu, wd):
    return (jax.nn.silu(x @ wg) * (x @ wu)) @ wd


def rel_bucket(n):
    n = jnp.maximum(n, 0)
    max_exact = N_BUCKETS // 2
    nf = jnp.maximum(n, 1).astype(jnp.float32)
    large = max_exact + (jnp.log(nf / max_exact) / math.log(MAX_DISTANCE / max_exact)
                         * (N_BUCKETS - max_exact)).astype(jnp.int32)
    large = jnp.minimum(large, N_BUCKETS - 1)
    return jnp.where(n < max_exact, n, large)


def to_blocks(t):
    b, s = t.shape[0], t.shape[1]
    return jnp.moveaxis(t.reshape((b, s // Q_BLOCK, Q_BLOCK) + t.shape[2:]), 1, 0)


def from_blocks(t):
    t = jnp.moveaxis(t, 0, 1)
    return t.reshape((t.shape[0], t.shape[1] * t.shape[2]) + t.shape[3:])


def pool_mixer(u, pool_w, pool_scale):
    b, s, _ = u.shape
    uf = u.astype(jnp.float32)
    c = lax.cumsum(uf, axis=1)
    c0 = jnp.concatenate([jnp.zeros_like(c[:, :1]), c[:, :-1]], axis=1)
    t = jnp.arange(s)
    outs = []
    for g, w in enumerate(POOL_WINDOWS):
        sl = slice(g * POOL_GROUP_DIM, (g + 1) * POOL_GROUP_DIM)
        lag = jnp.concatenate([jnp.zeros((b, w - 1, POOL_GROUP_DIM), jnp.float32),
                               c0[:, :s - w + 1, sl]], axis=1)
        cnt = jnp.minimum(t + 1, w).astype(jnp.float32)[None, :, None]
        d = (c[..., sl] - lag) / cnt - uf[..., sl]
        outs.append(d.astype(u.dtype) @ pool_w[g])
    return jnp.concatenate(outs, axis=-1) * pool_scale


def dsa_mixer(q, k, v, qi, ki, wi, pos, rel_bias, topk):
    def block(args):
        qb, qib, wib, pb = args
        sc = jnp.einsum('bqhd,bsd->bqhs', qib, ki) * (IDX_DIM ** -0.5)
        idx_score = jnp.einsum('bqhs,bqh->bqs', jax.nn.relu(sc), wib * (IDX_HEADS ** -0.5))
        admiss = pos[:, None, :] <= pb[:, :, None]
        idx_score = jnp.where(admiss, idx_score.astype(jnp.float32), -jnp.inf)
        _, sel = lax.top_k(idx_score, topk)
        sel_pos = jax.vmap(lambda p, ii: p[ii])(pos, sel)
        ks = jax.vmap(lambda kk, ii: kk[ii])(k, sel)
        vs = jax.vmap(lambda vv, ii: vv[ii])(v, sel)
        valid = sel_pos <= pb[:, :, None]
        logits = jnp.einsum('bqhd,bqkhd->bhqk', qb, ks).astype(jnp.float32) * (HEAD_DIM ** -0.5)
        bias = jnp.transpose(rel_bias[rel_bucket(pb[:, :, None] - sel_pos)], (0, 3, 1, 2))
        logits = jnp.where(valid[:, None], logits + bias.astype(jnp.float32), -jnp.inf)
        p = jax.nn.softmax(logits, axis=-1)
        return jnp.einsum('bhqk,bqkhd->bqhd', p.astype(v.dtype), vs)

    out = lax.map(block, (to_blocks(q), to_blocks(qi), to_blocks(wi), to_blocks(pos)))
    return from_blocks(out)


def stickbreak_mixer(q, k, v, pos):
    def block(args):
        qb, pb = args
        z = jnp.einsum('bqhd,bshd->bhqs', qb, k).astype(jnp.float32) * (HEAD_DIM ** -0.5)
        strict = (pos[:, None, :] < pb[:, :, None])[:, None]
        log_1m = jnp.where(strict, jax.nn.log_sigmoid(-z), 0.0)
        later = lax.cumsum(log_1m, axis=3, reverse=True) - log_1m
        a = jnp.where(strict, jnp.exp(jax.nn.log_sigmoid(z) + later), 0.0)
        return jnp.einsum('bhqs,bshd->bqhd', a.astype(v.dtype), v)

    out = lax.map(block, (to_blocks(q), to_blocks(pos)))
    return from_blocks(out)


def setup_inputs(seed: int = 0) -> dict:
    key = jax.random.key(seed)
    ks = jax.random.split(key, 20)
    f32 = jnp.float32
    nrm = lambda k, shape, scale: jax.random.normal(k, shape, f32) * scale
    gain = lambda k, shape: 1.0 + 0.02 * jax.random.normal(k, shape, f32)
    return {
        "x": jax.random.normal(ks[0], (BATCH, SEQ, D_MODEL), f32),
        "positions": jnp.broadcast_to(jnp.arange(SEQ, dtype=jnp.int32), (BATCH, SEQ)),
        "rel_bias": nrm(ks[1], (N_BUCKETS, DSA_HEADS), 0.5),
        "ffn1_norm": gain(ks[2], (DEPTH, D_MODEL)),
        "ffn1_gate": nrm(ks[3], (DEPTH, D_MODEL, D_FF), D_MODEL ** -0.5),
        "ffn1_up": nrm(ks[4], (DEPTH, D_MODEL, D_FF), D_MODEL ** -0.5),
        "ffn1_down": nrm(ks[5], (DEPTH, D_FF, D_MODEL), D_FF ** -0.5),
        "mix_norm": gain(ks[6], (DEPTH, D_MODEL)),
        "w_in": nrm(ks[7], (DEPTH, D_MODEL, IN_WIDTH), D_MODEL ** -0.5),
        "pool_w": nrm(ks[8], (DEPTH, POOL_GROUPS, POOL_GROUP_DIM, POOL_GROUP_DIM), POOL_GROUP_DIM ** -0.5),
        "pool_scale": 1.0 + 0.1 * jax.random.normal(ks[9], (DEPTH, POOL_WIDTH), f32),
        "q_norm": gain(ks[10], (DEPTH, HEAD_DIM)),
        "k_norm": gain(ks[11], (DEPTH, HEAD_DIM)),
        "w_out": nrm(ks[12], (DEPTH, MIX_WIDTH, D_MODEL), MIX_WIDTH ** -0.5),
        "ffn2_norm": gain(ks[13], (DEPTH, D_MODEL)),
        "ffn2_gate": nrm(ks[14], (DEPTH, D_MODEL, D_FF), D_MODEL ** -0.5),
        "ffn2_up": nrm(ks[15], (DEPTH, D_MODEL, D_FF), D_MODEL ** -0.5),
        "ffn2_down": nrm(ks[16], (DEPTH, D_FF, D_MODEL), D_FF ** -0.5),
    }


def reference(x, positions, rel_bias, ffn1_norm, ffn1_gate, ffn1_up, ffn1_down,
              mix_norm, w_in, pool_w, pool_scale, q_norm, k_norm, w_out,
              ffn2_norm, ffn2_gate, ffn2_up, ffn2_down):
    b, s, _ = x.shape
    topk = min(INDEX_TOPK, s // 4)
    offsets = [int(o) for o in np.cumsum(IN_SIZES)[:-1]]
    for i in range(DEPTH):
        h = x + 0.5 * swiglu(rmsnorm(x, ffn1_norm[i]), ffn1_gate[i], ffn1_up[i], ffn1_down[i])
        u = rmsnorm(h, mix_norm[i]) @ w_in[i]
        (u_pool, qb, kb, vb, qi, ki, wi, qc, kc, vc) = jnp.split(u, offsets, axis=-1)
        y_pool = pool_mixer(u_pool, pool_w[i], pool_scale[i])
        qb = rmsnorm(qb.reshape(b, s, DSA_HEADS, HEAD_DIM), q_norm[i])
        kb = rmsnorm(kb.reshape(b, s, DSA_HEADS, HEAD_DIM), k_norm[i])
        vb = vb.reshape(b, s, DSA_HEADS, HEAD_DIM)
        y_dsa = dsa_mixer(qb, kb, vb, qi.reshape(b, s, IDX_HEADS, IDX_DIM), ki, wi,
                          positions, rel_bias, topk).reshape(b, s, DSA_WIDTH)
        y_sb = stickbreak_mixer(qc.reshape(b, s, SB_HEADS, HEAD_DIM),
                                kc.reshape(b, s, SB_HEADS, HEAD_DIM),
                                vc.reshape(b, s, SB_HEADS, HEAD_DIM),
                                positions).reshape(b, s, SB_WIDTH)
        h = h + jnp.concatenate([y_pool, y_dsa, y_sb], axis=-1) @ w_out[i]
        x = h + 0.5 * swiglu(rmsnorm(h, ffn2_norm[i]), ffn2_gate[i], ffn2_up[i], ffn2_down[i])
    return x
```

```python
import functools
import math

import jax
import jax.numpy as jnp
import numpy as np
from jax import lax
from jax.experimental import pallas as pl
from jax.experimental.pallas import tpu as pltpu

HEAD_DIM = 128
POOL_WINDOWS = (2, 4, 8, 16)
IDX_HEADS = 16
IDX_DIM = 64
INDEX_TOPK = 256
N_BUCKETS = 32
MAX_DISTANCE = 128
EPS = 1e-6

LANES = 128
POOL_HISTORY = 16
NEG = -1e30
INT_MIN = -(2 ** 31)
VMEM_LIMIT = 56 * 1024 * 1024

F32 = jnp.float32
BF16 = jnp.bfloat16


def _cparams(n_grid):
    return pltpu.CompilerParams(dimension_semantics=("arbitrary",) * n_grid,
                                vmem_limit_bytes=VMEM_LIMIT)


def _pick(n, pref):
    t = min(pref, n)
    while n % t:
        t //= 2
    return t


def _rmsnorm_kernel(x_ref, g_ref, o_ref):
    x = x_ref[...]
    ms = jnp.mean(x * x, axis=-1, keepdims=True)
    o_ref[...] = (x * lax.rsqrt(ms + EPS) * g_ref[...]).astype(o_ref.dtype)


def _rmsnorm(x, g):
    s, d = x.shape
    tm = _pick(s, 256)
    return pl.pallas_call(
        _rmsnorm_kernel,
        out_shape=jax.ShapeDtypeStruct((s, d), BF16),
        grid=(s // tm,),
        in_specs=[pl.BlockSpec((tm, d), lambda i: (i, 0)),
                  pl.BlockSpec((1, d), lambda i: (0, 0))],
        out_specs=pl.BlockSpec((tm, d), lambda i: (i, 0)),
        compiler_params=_cparams(1),
        name="rmsnorm",
    )(x, g.reshape(1, d))


def _mm_kernel(*refs, n_a, n_b, mode, res_scale):
    a_refs = refs[:n_a]
    b_refs = refs[n_a:n_a + n_b]
    extra = refs[n_a + n_b:-1]
    o_ref = refs[-1]
    n_pairs = n_b

    def dot(a_ref, b_ref):
        return jnp.dot(a_ref[...], b_ref[...], preferred_element_type=F32)

    if mode == "swiglu":
        g = dot(a_refs[0], b_refs[0])
        u = dot(a_refs[0], b_refs[1])
        o_ref[...] = (g / (1.0 + jnp.exp(-g)) * u).astype(o_ref.dtype)
        return
    acc = dot(a_refs[0], b_refs[0])
    for p in range(1, n_pairs):
        acc = acc + dot(a_refs[p], b_refs[p])
    if mode == "cast":
        o_ref[...] = acc.astype(o_ref.dtype)
    elif mode == "residual":
        o_ref[...] = extra[0][...] + res_scale * acc
    elif mode == "headnorm":
        gain = extra[0][...]
        for c in range(acc.shape[1] // HEAD_DIM):
            sl = slice(c * HEAD_DIM, (c + 1) * HEAD_DIM)
            y = acc[:, sl]
            ms = jnp.mean(y * y, axis=-1, keepdims=True)
            o_ref[:, sl] = (y * lax.rsqrt(ms + EPS) * gain[:, sl]).astype(o_ref.dtype)
    else:
        raise ValueError(mode)


def _matmul(a_list, b_list, *, mode, out_dtype, tm, tn, extra=None, res_scale=1.0, name):
    m = a_list[0].shape[0]
    n = b_list[0].shape[1]
    tm, tn = _pick(m, tm), _pick(n, tn)
    if mode == "swiglu":
        a_in = [a_list[0]]
    else:
        a_in = list(a_list)
    in_specs = [pl.BlockSpec((tm, a.shape[1]), lambda i, j: (i, 0)) for a in a_in]
    in_specs += [pl.BlockSpec((b.shape[0], tn), lambda i, j: (0, j)) for b in b_list]
    operands = a_in + list(b_list)
    if mode == "residual":
        in_specs.append(pl.BlockSpec((tm, tn), lambda i, j: (i, j)))
        operands.append(extra)
    elif mode == "headnorm":
        in_specs.append(pl.BlockSpec((1, tn), lambda i, j: (0, j)))
        operands.append(extra)
    return pl.pallas_call(
        functools.partial(_mm_kernel, n_a=len(a_in), n_b=len(b_list), mode=mode, res_scale=res_scale),
        out_shape=jax.ShapeDtypeStruct((m, n), out_dtype),
        grid=(m // tm, n // tn),
        in_specs=in_specs,
        out_specs=pl.BlockSpec((tm, tn), lambda i, j: (i, j)),
        compiler_params=_cparams(2),
        name=name,
    )(*operands)


def _pool_kernel(cur_ref, prev_ref, w_ref, scale_ref, o_ref, *, tm, gd):
    i = pl.program_id(0)
    cur = cur_ref[...]
    prev = jnp.where(i > 0, prev_ref[...], 0.0)
    ext = jnp.concatenate([prev, cur], axis=0)
    t = i * tm + lax.broadcasted_iota(jnp.int32, (tm, 1), 0)
    sums = {1: ext}
    w = 1
    while w < max(POOL_WINDOWS):
        sums[2 * w] = sums[w] + pltpu.roll(sums[w], w, axis=0)
        w *= 2
    for g, win in enumerate(POOL_WINDOWS):
        sl = slice(g * gd, (g + 1) * gd)
        cnt = jnp.minimum(t + 1, win).astype(F32)
        d = sums[win][POOL_HISTORY:, sl] / cnt - cur[:, sl]
        y = jnp.dot(d.astype(BF16), w_ref[g], preferred_element_type=F32)
        o_ref[:, sl] = (y * scale_ref[:, sl]).astype(o_ref.dtype)


def _pool_mixer(u_pool, pool_w, pool_scale):
    s, pw = u_pool.shape
    groups, gd, _ = pool_w.shape
    tm = _pick(s, 512)
    hist_blocks = tm // POOL_HISTORY
    return pl.pallas_call(
        functools.partial(_pool_kernel, tm=tm, gd=gd),
        out_shape=jax.ShapeDtypeStruct((s, pw), BF16),
        grid=(s // tm,),
        in_specs=[pl.BlockSpec((tm, pw), lambda i: (i, 0)),
                  pl.BlockSpec((POOL_HISTORY, pw), lambda i: (jnp.maximum(i * hist_blocks - 1, 0), 0)),
                  pl.BlockSpec((groups, gd, gd), lambda i: (0, 0, 0)),
                  pl.BlockSpec((1, pw), lambda i: (0, 0))],
        out_specs=pl.BlockSpec((tm, pw), lambda i: (i, 0)),
        compiler_params=_cparams(1),
        name="pool_mixer",
    )(u_pool, u_pool, pool_w.astype(BF16), pool_scale.reshape(1, pw))


def _dsa_index_kernel(qi_ref, kit_ref, w_ref, o_ref, keys_ref, wb_ref, *, t, topk, rows):
    qb = pl.program_id(0)
    nkb = o_ref.shape[0]
    w = w_ref[...] * (IDX_DIM ** -0.5 * IDX_HEADS ** -0.5)
    for h in range(IDX_HEADS):
        wb_ref[h] = jnp.broadcast_to(w[:, h:h + 1], (t, t))
    row = qb * t + lax.broadcasted_iota(jnp.int32, (t, t), 0)
    col0 = lax.broadcasted_iota(jnp.int32, (t, t), 1)

    def score_tile(kb, carry):
        kt = kit_ref[kb]
        acc = jnp.zeros((t, t), F32)
        for h in range(IDX_HEADS):
            sc = jnp.dot(qi_ref[h], kt, preferred_element_type=F32)
            acc = acc + jnp.maximum(sc, 0.0) * wb_ref[h]
        bits = pltpu.bitcast(acc, jnp.int32)
        key = jnp.where(bits < 0, bits ^ 0x7FFFFFFF, bits)
        keys_ref[kb] = jnp.where(kb * t + col0 <= row, key, INT_MIN)
        return carry

    lax.fori_loop(0, qb + 1, score_tile, 0)

    thr_parts = []
    for r0 in range(0, t, rows):
        rsl = slice(r0, r0 + rows)

        def bit_step(it, thr):
            cand = thr + jnp.left_shift(jnp.int32(1), 31 - it)
            cand_b = jnp.broadcast_to(cand, (rows, LANES))

            def count_tile(kb, c):
                for j in range(t // LANES):
                    kk = keys_ref[kb, rsl, j * LANES:(j + 1) * LANES]
                    c = c + jnp.where(kk >= cand_b, 1, 0)
                return c

            c = lax.fori_loop(0, qb + 1, count_tile, jnp.zeros((rows, LANES), jnp.int32))
            cnt = jnp.sum(c.astype(F32), axis=1, keepdims=True)
            return jnp.where(cnt >= float(topk), cand, thr)

        thr_parts.append(lax.fori_loop(0, 32, bit_step, jnp.full((rows, 1), INT_MIN, jnp.int32)))
    thr = jnp.concatenate(thr_parts, axis=0)
    thr_b = jnp.broadcast_to(jnp.maximum(thr, INT_MIN + 1), (t, t))

    def write_tile(kb, carry):
        o_ref[kb] = jnp.where(keys_ref[kb] >= thr_b, 0.0, NEG).astype(o_ref.dtype)
        return carry

    def fill_tile(kb, carry):
        o_ref[kb] = jnp.full((t, t), NEG, o_ref.dtype)
        return carry

    lax.fori_loop(0, qb + 1, write_tile, 0)
    lax.fori_loop(qb + 1, nkb, fill_tile, 0)


def _dsa_index(qi_hm, kit, wi, *, t, topk):
    heads, s, idim = qi_hm.shape
    nkb = s // t
    return pl.pallas_call(
        functools.partial(_dsa_index_kernel, t=t, topk=topk, rows=min(t, 128)),
        out_shape=jax.ShapeDtypeStruct((nkb, s, t), BF16),
        grid=(nkb,),
        in_specs=[pl.BlockSpec((heads, t, idim), lambda i: (0, i, 0)),
                  pl.BlockSpec((nkb, idim, t), lambda i: (0, 0, 0)),
                  pl.BlockSpec((t, heads), lambda i: (i, 0))],
        out_specs=pl.BlockSpec((nkb, t, t), lambda i: (0, i, 0)),
        scratch_shapes=[pltpu.VMEM((nkb, t, t), jnp.int32),
                        pltpu.VMEM((heads, t, t), F32)],
        compiler_params=_cparams(1),
        name="dsa_index",
    )(qi_hm, kit, wi)


def _dsa_attn_kernel(cfar_ref, q_ref, k_ref, v_ref, mask_ref, tdiag_ref, toff_ref, o_ref, *, t):
    h = pl.program_id(0)
    qb = pl.program_id(1)
    q = q_ref[...]
    scale = HEAD_DIM ** -0.5

    def tile(kb, bias, carry):
        m, l, acc = carry
        rows = pl.ds(pl.multiple_of(kb * t, t), t)
        s = lax.dot_general(q, k_ref[rows, :], (((1,), (1,)), ((), ())), preferred_element_type=F32)
        s = s * scale + bias + mask_ref[kb].astype(F32)
        m_new = jnp.maximum(m, jnp.max(s, axis=1, keepdims=True))
        alpha = jnp.exp(m - m_new)
        p = jnp.exp(s - m_new)
        l = alpha * l + jnp.sum(p, axis=1, keepdims=True)
        acc = alpha * acc + jnp.dot(p.astype(BF16), v_ref[rows, :], preferred_element_type=F32)
        return m_new, l, acc

    carry = (jnp.full((t, 1), NEG, F32), jnp.zeros((t, 1), F32), jnp.zeros((t, HEAD_DIM), F32))
    cfar = cfar_ref[h]
    carry = lax.fori_loop(0, jnp.maximum(qb - 1, 0), lambda kb, c: tile(kb, cfar, c), carry)
    carry = lax.fori_loop(jnp.maximum(qb - 1, 0), qb, lambda kb, c: tile(kb, toff_ref[0], c), carry)
    _, l, acc = tile(qb, tdiag_ref[0], carry)
    o_ref[...] = (acc / l).astype(o_ref.dtype)


def _dsa_attn(qk, rest, mask, tdiag, toff, cfar, *, t, heads):
    s = qk.shape[0]
    nkb = s // t
    return pl.pallas_call(
        functools.partial(_dsa_attn_kernel, t=t),
        out_shape=jax.ShapeDtypeStruct((s, heads * HEAD_DIM), BF16),
        grid=(heads, nkb),
        in_specs=[pl.BlockSpec(memory_space=pltpu.SMEM),
                  pl.BlockSpec((t, HEAD_DIM), lambda h, i: (i, h)),
                  pl.BlockSpec((s, HEAD_DIM), lambda h, i: (0, heads + h)),
                  pl.BlockSpec((s, HEAD_DIM), lambda h, i: (0, h)),
                  pl.BlockSpec((nkb, t, t), lambda h, i: (0, i, 0)),
                  pl.BlockSpec((1, t, t), lambda h, i: (h, 0, 0)),
                  pl.BlockSpec((1, t, t), lambda h, i: (h, 0, 0))],
        out_specs=pl.BlockSpec((t, HEAD_DIM), lambda h, i: (i, h)),
        compiler_params=_cparams(2),
        name="dsa_attn",
    )(cfar, qk, qk, rest, mask, tdiag, toff)


def _sb_kernel(q_ref, k_ref, v_ref, tri_ref, o_ref, *, t, tk):
    qb = pl.program_id(1)
    q = q_ref[...]
    tri = tri_ref[...]
    scale = HEAD_DIM ** -0.5
    per = t // tk
    row = qb * t + lax.broadcasted_iota(jnp.int32, (t, tk), 0)
    col0 = lax.broadcasted_iota(jnp.int32, (t, tk), 1)

    def tile(kb, strict, carry):
        later0, acc = carry
        rows = pl.ds(pl.multiple_of(kb * tk, tk), tk)
        z = lax.dot_general(q, k_ref[rows, :], (((1,), (1,)), ((), ())), preferred_element_type=F32) * scale
        sp = jnp.maximum(z, 0.0) + jnp.log(1.0 + jnp.exp(-jnp.abs(z)))
        l1m = -sp
        if strict is not None:
            l1m = jnp.where(strict(kb), l1m, 0.0)
        hi = l1m.astype(BF16)
        lo = (l1m - hi.astype(F32)).astype(BF16)
        cs = jnp.dot(hi, tri, preferred_element_type=F32) + jnp.dot(lo, tri, preferred_element_type=F32)
        a = jnp.exp(z - sp + later0 + cs[:, :tk])
        if strict is not None:
            a = jnp.where(strict(kb), a, 0.0)
        acc = acc + jnp.dot(a.astype(BF16), v_ref[rows, :], preferred_element_type=F32)
        return later0 + cs[:, tk:], acc

    carry = (jnp.zeros((t, tk), F32), jnp.zeros((t, HEAD_DIM), F32))
    for d in range(per):
        carry = tile(qb * per + (per - 1 - d), lambda kb: kb * tk + col0 < row, carry)
    carry = lax.fori_loop(0, qb * per, lambda i, c: tile(qb * per - 1 - i, None, c), carry)
    o_ref[...] = carry[1].astype(o_ref.dtype)


def _sb_attn(rest, tri, *, t, tk, heads, q_blk, k_blk, v_blk):
    s = rest.shape[0]
    return pl.pallas_call(
        functools.partial(_sb_kernel, t=t, tk=tk),
        out_shape=jax.ShapeDtypeStruct((s, heads * HEAD_DIM), BF16),
        grid=(heads, s // t),
        in_specs=[pl.BlockSpec((t, HEAD_DIM), lambda h, i: (i, q_blk + h)),
                  pl.BlockSpec((s, HEAD_DIM), lambda h, i: (0, k_blk + h)),
                  pl.BlockSpec((s, HEAD_DIM), lambda h, i: (0, v_blk + h)),
                  pl.BlockSpec((tk, 2 * tk), lambda h, i: (0, 0))],
        out_specs=pl.BlockSpec((t, HEAD_DIM), lambda h, i: (i, h)),
        compiler_params=_cparams(2),
        name="sb_attn",
    )(rest, rest, rest, tri)


def _rel_bucket(n):
    max_exact = N_BUCKETS // 2
    nf = jnp.maximum(n, 1).astype(F32)
    large = max_exact + (jnp.log(nf / max_exact) / math.log(MAX_DISTANCE / max_exact)
                         * (N_BUCKETS - max_exact)).astype(jnp.int32)
    large = jnp.minimum(large, N_BUCKETS - 1)
    return jnp.where(n < max_exact, n, large)


def _bias_tables(rel_bias, t):
    assert t >= MAX_DISTANCE
    by_dist = rel_bias[_rel_bucket(jnp.arange(2 * t))]
    r = jnp.arange(t)[:, None]
    c = jnp.arange(t)[None, :]
    tdiag = jnp.transpose(by_dist[jnp.maximum(r - c, 0)], (2, 0, 1))
    toff = jnp.transpose(by_dist[t + r - c], (2, 0, 1))
    return tdiag, toff, rel_bias[N_BUCKETS - 1]


def kernel(x, positions, rel_bias, ffn1_norm, ffn1_gate, ffn1_up, ffn1_down, mix_norm, w_in, pool_w,
           pool_scale, q_norm, k_norm, w_out, ffn2_norm, ffn2_gate, ffn2_up, ffn2_down):
    del positions
    b, s, d = x.shape
    assert b == 1
    depth = w_in.shape[0]
    pool_width = pool_w.shape[1] * pool_w.shape[2]
    dsa_width = (w_in.shape[2] - pool_width - IDX_HEADS * IDX_DIM - IDX_DIM - IDX_HEADS) // 6
    sb_width = dsa_width
    dsa_heads = dsa_width // HEAD_DIM
    sb_heads = sb_width // HEAD_DIM
    topk = min(INDEX_TOPK, s // 4)
    t = _pick(s, 256)
    tk_sb = 128

    o = np.cumsum([0, pool_width, dsa_width, dsa_width, dsa_width, IDX_HEADS * IDX_DIM, IDX_DIM, IDX_HEADS,
                   sb_width, sb_width, sb_width])
    small_pad = LANES - IDX_DIM - IDX_HEADS

    tdiag, toff, cfar = _bias_tables(rel_bias, t)
    jj = jnp.arange(tk_sb)
    tri = jnp.concatenate([(jj[:, None] > jj[None, :]), jnp.ones((tk_sb, tk_sb), bool)], axis=1).astype(BF16)

    def ffn(xin, norm, wg, wu, wd):
        xn = _rmsnorm(xin, norm)
        act = _matmul([xn], [wg.astype(BF16), wu.astype(BF16)], mode="swiglu", out_dtype=BF16,
                      tm=1024, tn=512, name="ffn_up")
        return _matmul([act], [wd.astype(BF16)], mode="residual", out_dtype=F32, tm=512, tn=512,
                       extra=xin, res_scale=0.5, name="ffn_down")

    h = x[0]
    for i in range(depth):
        h = ffn(h, ffn1_norm[i], ffn1_gate[i], ffn1_up[i], ffn1_down[i])

        hn = _rmsnorm(h, mix_norm[i])
        wi_l = w_in[i]
        w_pool = wi_l[:, o[0]:o[1]].astype(BF16)
        w_qk = wi_l[:, o[1]:o[3]].astype(BF16)
        w_rest = jnp.concatenate([wi_l[:, o[3]:o[5]], wi_l[:, o[7]:o[10]]], axis=1).astype(BF16)
        w_small = jnp.pad(wi_l[:, o[5]:o[7]], ((0, 0), (0, small_pad))).astype(BF16)
        gain = jnp.concatenate([jnp.tile(q_norm[i], dsa_heads), jnp.tile(k_norm[i], dsa_heads)])[None, :]

        u_pool = _matmul([hn], [w_pool], mode="cast", out_dtype=F32, tm=1024, tn=512, name="proj_pool")
        qk = _matmul([hn], [w_qk], mode="headnorm", out_dtype=BF16, tm=1024, tn=512, extra=gain,
                     name="proj_qk")
        rest = _matmul([hn], [w_rest], mode="cast", out_dtype=BF16, tm=1024, tn=512, name="proj_rest")
        small = _matmul([hn], [w_small], mode="cast", out_dtype=F32, tm=1024, tn=LANES, name="proj_small")

        y_pool = _pool_mixer(u_pool, pool_w[i], pool_scale[i])

        qi_hm = jnp.transpose(rest[:, dsa_width:dsa_width + IDX_HEADS * IDX_DIM].reshape(s, IDX_HEADS, IDX_DIM),
                              (1, 0, 2))
        kit = jnp.transpose(small[:, :IDX_DIM].astype(BF16).reshape(s // t, t, IDX_DIM), (0, 2, 1))
        wi = small[:, IDX_DIM:IDX_DIM + IDX_HEADS]
        mask = _dsa_index(qi_hm, kit, wi, t=t, topk=topk)
        y_dsa = _dsa_attn(qk, rest, mask, tdiag, toff, cfar, t=t, heads=dsa_heads)

        sb0 = (dsa_width + IDX_HEADS * IDX_DIM) // HEAD_DIM
        y_sb = _sb_attn(rest, tri, t=t, tk=tk_sb, heads=sb_heads, q_blk=sb0, k_blk=sb0 + sb_heads,
                        v_blk=sb0 + 2 * sb_heads)

        wo = w_out[i].astype(BF16)
        p1, p2 = pool_width, pool_width + dsa_width
        h = _matmul([y_pool, y_dsa, y_sb], [wo[:p1], wo[p1:p2], wo[p2:]], mode="residual", out_dtype=F32,
                    tm=1024, tn=512, extra=h, name="mix_out")

        h = ffn(h, ffn2_norm[i], ffn2_gate[i], ffn2_up[i], ffn2_down[i])
    return h[None]
```

```python
import functools
import math

import jax
import jax.numpy as jnp
import numpy as np
from jax import lax
from jax.experimental import pallas as pl
from jax.experimental.pallas import tpu as pltpu

HEAD_DIM = 128
POOL_WINDOWS = (2, 4, 8, 16)
IDX_HEADS = 16
IDX_DIM = 64
INDEX_TOPK = 256
N_BUCKETS = 32
MAX_DISTANCE = 128
EPS = 1e-6

LANES = 128
POOL_HISTORY = 16
NEG = -1e30
INT_MIN = -(2 ** 31)
VMEM_LIMIT = 56 * 1024 * 1024

F32 = jnp.float32
BF16 = jnp.bfloat16


def _cparams(n_grid):
    return pltpu.CompilerParams(dimension_semantics=("arbitrary",) * n_grid,
                                vmem_limit_bytes=VMEM_LIMIT)


def _pick(n, pref):
    t = min(pref, n)
    while n % t:
        t //= 2
    return t


def _rmsnorm_kernel(x_ref, g_ref, o_ref):
    x = x_ref[...]
    ms = jnp.mean(x * x, axis=-1, keepdims=True)
    o_ref[...] = (x * lax.rsqrt(ms + EPS) * g_ref[...]).astype(o_ref.dtype)


def _rmsnorm(x, g):
    s, d = x.shape
    tm = _pick(s, 256)
    return pl.pallas_call(
        _rmsnorm_kernel,
        out_shape=jax.ShapeDtypeStruct((s, d), BF16),
        grid=(s // tm,),
        in_specs=[pl.BlockSpec((tm, d), lambda i: (i, 0)),
                  pl.BlockSpec((1, d), lambda i: (0, 0))],
        out_specs=pl.BlockSpec((tm, d), lambda i: (i, 0)),
        compiler_params=_cparams(1),
        name="rmsnorm",
    )(x, g.reshape(1, d))


def _mm_kernel(*refs, n_a, n_b, mode, res_scale):
    a_refs = refs[:n_a]
    b_refs = refs[n_a:n_a + n_b]
    extra = refs[n_a + n_b:-1]
    o_ref = refs[-1]
    n_pairs = n_b

    def dot(a_ref, b_ref):
        return jnp.dot(a_ref[...], b_ref[...], preferred_element_type=F32)

    if mode == "swiglu":
        g = dot(a_refs[0], b_refs[0])
        u = dot(a_refs[0], b_refs[1])
        o_ref[...] = (g / (1.0 + jnp.exp(-g)) * u).astype(o_ref.dtype)
        return
    acc = dot(a_refs[0], b_refs[0])
    for p in range(1, n_pairs):
        acc = acc + dot(a_refs[p], b_refs[p])
    if mode == "cast":
        o_ref[...] = acc.astype(o_ref.dtype)
    elif mode == "residual":
        o_ref[...] = extra[0][...] + res_scale * acc
    elif mode == "headnorm":
        gain = extra[0][...]
        for c in range(acc.shape[1] // HEAD_DIM):
            sl = slice(c * HEAD_DIM, (c + 1) * HEAD_DIM)
            y = acc[:, sl]
            ms = jnp.mean(y * y, axis=-1, keepdims=True)
            o_ref[:, sl] = (y * lax.rsqrt(ms + EPS) * gain[:, sl]).astype(o_ref.dtype)
    else:
        raise ValueError(mode)


def _matmul(a_list, b_list, *, mode, out_dtype, tm, tn, extra=None, res_scale=1.0, name):
    m = a_list[0].shape[0]
    n = b_list[0].shape[1]
    tm, tn = _pick(m, tm), _pick(n, tn)
    if mode == "swiglu":
        a_in = [a_list[0]]
    else:
        a_in = list(a_list)
    in_specs = [pl.BlockSpec((tm, a.shape[1]), lambda i, j: (i, 0)) for a in a_in]
    in_specs += [pl.BlockSpec((b.shape[0], tn), lambda i, j: (0, j)) for b in b_list]
    operands = a_in + list(b_list)
    if mode == "residual":
        in_specs.append(pl.BlockSpec((tm, tn), lambda i, j: (i, j)))
        operands.append(extra)
    elif mode == "headnorm":
        in_specs.append(pl.BlockSpec((1, tn), lambda i, j: (0, j)))
        operands.append(extra)
    return pl.pallas_call(
        functools.partial(_mm_kernel, n_a=len(a_in), n_b=len(b_list), mode=mode, res_scale=res_scale),
        out_shape=jax.ShapeDtypeStruct((m, n), out_dtype),
        grid=(m // tm, n // tn),
        in_specs=in_specs,
        out_specs=pl.BlockSpec((tm, tn), lambda i, j: (i, j)),
        compiler_params=_cparams(2),
        name=name,
    )(*operands)


def _pool_kernel(cur_ref, prev_ref, w_ref, scale_ref, o_ref, *, tm, gd):
    i = pl.program_id(0)
    cur = cur_ref[...]
    prev = jnp.where(i > 0, prev_ref[...], 0.0)
    ext = jnp.concatenate([prev, cur], axis=0)
    t = i * tm + lax.broadcasted_iota(jnp.int32, (tm, 1), 0)
    sums = {1: ext}
    w = 1
    while w < max(POOL_WINDOWS):
        sums[2 * w] = sums[w] + pltpu.roll(sums[w], w, axis=0)
        w *= 2
    for g, win in enumerate(POOL_WINDOWS):
        sl = slice(g * gd, (g + 1) * gd)
        cnt = jnp.minimum(t + 1, win).astype(F32)
        d = sums[win][POOL_HISTORY:, sl] / cnt - cur[:, sl]
        y = jnp.dot(d.astype(BF16), w_ref[g], preferred_element_type=F32)
        o_ref[:, sl] = (y * scale_ref[:, sl]).astype(o_ref.dtype)


def _pool_mixer(u_pool, pool_w, pool_scale):
    s, pw = u_pool.shape
    groups, gd, _ = pool_w.shape
    tm = _pick(s, 512)
    hist_blocks = tm // POOL_HISTORY
    return pl.pallas_call(
        functools.partial(_pool_kernel, tm=tm, gd=gd),
        out_shape=jax.ShapeDtypeStruct((s, pw), BF16),
        grid=(s // tm,),
        in_specs=[pl.BlockSpec((tm, pw), lambda i: (i, 0)),
                  pl.BlockSpec((POOL_HISTORY, pw), lambda i: (jnp.maximum(i * hist_blocks - 1, 0), 0)),
                  pl.BlockSpec((groups, gd, gd), lambda i: (0, 0, 0)),
                  pl.BlockSpec((1, pw), lambda i: (0, 0))],
        out_specs=pl.BlockSpec((tm, pw), lambda i: (i, 0)),
        compiler_params=_cparams(1),
        name="pool_mixer",
    )(u_pool, u_pool, pool_w.astype(BF16), pool_scale.reshape(1, pw))


def _dsa_index_kernel(qi_ref, kit_ref, w_ref, o_ref, keys_ref, wb_ref, *, t, topk, rows):
    qb = pl.program_id(0)
    nkb = o_ref.shape[0]
    w = w_ref[...] * (IDX_DIM ** -0.5 * IDX_HEADS ** -0.5)
    for h in range(IDX_HEADS):
        wb_ref[h] = jnp.broadcast_to(w[:, h:h + 1], (t, t))
    row = qb * t + lax.broadcasted_iota(jnp.int32, (t, t), 0)
    col0 = lax.broadcasted_iota(jnp.int32, (t, t), 1)

    def score_tile(kb, carry):
        kt = kit_ref[kb]
        acc = jnp.zeros((t, t), F32)
        for h in range(IDX_HEADS):
            sc = jnp.dot(qi_ref[h], kt, preferred_element_type=F32)
            acc = acc + jnp.maximum(sc, 0.0) * wb_ref[h]
        bits = pltpu.bitcast(acc, jnp.int32)
        key = jnp.where(bits < 0, bits ^ 0x7FFFFFFF, bits)
        keys_ref[kb] = jnp.where(kb * t + col0 <= row, key, INT_MIN)
        return carry

    lax.fori_loop(0, qb + 1, score_tile, 0)

    thr_parts = []
    for r0 in range(0, t, rows):
        rsl = slice(r0, r0 + rows)

        def bit_step(it, thr):
            cand = thr + jnp.left_shift(jnp.int32(1), 31 - it)
            cand_b = jnp.broadcast_to(cand, (rows, LANES))

            def count_tile(kb, c):
                for j in range(t // LANES):
                    kk = keys_ref[kb, rsl, j * LANES:(j + 1) * LANES]
                    c = c + jnp.where(kk >= cand_b, 1, 0)
                return c

            c = lax.fori_loop(0, qb + 1, count_tile, jnp.zeros((rows, LANES), jnp.int32))
            cnt = jnp.sum(c.astype(F32), axis=1, keepdims=True)
            return jnp.where(cnt >= float(topk), cand, thr)

        thr_parts.append(lax.fori_loop(0, 32, bit_step, jnp.full((rows, 1), INT_MIN, jnp.int32)))
    thr = jnp.concatenate(thr_parts, axis=0)
    thr_b = jnp.broadcast_to(jnp.maximum(thr, INT_MIN + 1), (t, t))

    def write_tile(kb, carry):
        o_ref[kb] = jnp.where(keys_ref[kb] >= thr_b, 0.0, NEG).astype(o_ref.dtype)
        return carry

    def fill_tile(kb, carry):
        o_ref[kb] = jnp.full((t, t), NEG, o_ref.dtype)
        return carry

    lax.fori_loop(0, qb + 1, write_tile, 0)
    lax.fori_loop(qb + 1, nkb, fill_tile, 0)


def _dsa_index(qi_hm, kit, wi, *, t, topk):
    heads, s, idim = qi_hm.shape
    nkb = s // t
    return pl.pallas_call(
        functools.partial(_dsa_index_kernel, t=t, topk=topk, rows=min(t, 128)),
        out_shape=jax.ShapeDtypeStruct((nkb, s, t), BF16),
        grid=(nkb,),
        in_specs=[pl.BlockSpec((heads, t, idim), lambda i: (0, i, 0)),
                  pl.BlockSpec((nkb, idim, t), lambda i: (0, 0, 0)),
                  pl.BlockSpec((t, heads), lambda i: (i, 0))],
        out_specs=pl.BlockSpec((nkb, t, t), lambda i: (0, i, 0)),
        scratch_shapes=[pltpu.VMEM((nkb, t, t), jnp.int32),
                        pltpu.VMEM((heads, t, t), F32)],
        compiler_params=_cparams(1),
        name="dsa_index",
    )(qi_hm, kit, wi)


def _dsa_attn_kernel(q_ref, k_ref, v_ref, mask_ref, tdiag_ref, toff_ref, o_ref, *, t, tm):
    qb = pl.program_id(1)
    q = q_ref[...]
    scale2 = HEAD_DIM ** -0.5 * math.log2(math.e)
    per = t // tm

    def chunk(kc, bias, carry):
        m, l, acc = carry
        rows = pl.ds(pl.multiple_of(kc * t, t), t)
        s = lax.dot_general(q, k_ref[rows, :], (((1,), (1,)), ((), ())), preferred_element_type=F32) * scale2
        s = s + jnp.concatenate([mask_ref[kc * per + j] for j in range(per)], axis=1).astype(F32)
        if bias is not None:
            s = s + bias[0]
        m_new = jnp.maximum(m, jnp.max(s, axis=1, keepdims=True))
        alpha = jnp.exp2(m - m_new)
        p = jnp.exp2(s - m_new)
        l = alpha * l + jnp.sum(p, axis=1, keepdims=True)
        acc = alpha * acc + jnp.dot(p.astype(BF16), v_ref[rows, :], preferred_element_type=F32)
        return m_new, l, acc

    carry = (jnp.full((t, 1), NEG, F32), jnp.zeros((t, 1), F32), jnp.zeros((t, HEAD_DIM), F32))
    carry = lax.fori_loop(0, jnp.maximum(qb - 1, 0), lambda kc, c: chunk(kc, None, c), carry)
    carry = lax.fori_loop(jnp.maximum(qb - 1, 0), qb, lambda kc, c: chunk(kc, toff_ref, c), carry)
    _, l, acc = chunk(qb, tdiag_ref, carry)
    o_ref[...] = (acc / l).astype(o_ref.dtype)


def _dsa_attn(qk, rest, mask, tdiag, toff, *, t, heads):
    s = qk.shape[0]
    nkb, _, tm = mask.shape
    return pl.pallas_call(
        functools.partial(_dsa_attn_kernel, t=t, tm=tm),
        out_shape=jax.ShapeDtypeStruct((s, heads * HEAD_DIM), BF16),
        grid=(heads, s // t),
        in_specs=[pl.BlockSpec((t, HEAD_DIM), lambda h, i: (i, h)),
                  pl.BlockSpec((s, HEAD_DIM), lambda h, i: (0, heads + h)),
                  pl.BlockSpec((s, HEAD_DIM), lambda h, i: (0, h)),
                  pl.BlockSpec((nkb, t, tm), lambda h, i: (0, i, 0)),
                  pl.BlockSpec((1, t, t), lambda h, i: (h, 0, 0)),
                  pl.BlockSpec((1, t, t), lambda h, i: (h, 0, 0))],
        out_specs=pl.BlockSpec((t, HEAD_DIM), lambda h, i: (i, h)),
        compiler_params=_cparams(2),
        name="dsa_attn",
    )(qk, qk, rest, mask, tdiag, toff)


def _sb_kernel(q_ref, k_ref, v_ref, tri_ref, o_ref, *, t, tk, unroll):
    qb = pl.program_id(1)
    q = q_ref[...]
    tri = tri_ref[...]
    scale2 = HEAD_DIM ** -0.5 * math.log2(math.e)
    per = t // tk
    assert unroll % per == 0

    def chunk(kb0, n, masked, carry):
        before, acc = carry
        width = n * tk
        rows = pl.ds(pl.multiple_of(kb0 * tk, tk), width)
        z = lax.dot_general(q, k_ref[rows, :], (((1,), (1,)), ((), ())), preferred_element_type=F32) * scale2
        lo_z, hi_z = jnp.minimum(z, 0.0), jnp.maximum(z, 0.0)
        lg = jnp.log2(1.0 + jnp.exp2(lo_z - hi_z))
        sp = hi_z + lg
        log_beta = lo_z - lg
        if masked:
            strict = kb0 * tk + lax.broadcasted_iota(jnp.int32, (t, width), 1) < row
            sp = jnp.where(strict, sp, 0.0)
        hi = sp.astype(BF16)
        lo = (sp - hi.astype(F32)).astype(BF16)
        cs = []
        for u in range(n):
            sl = slice(u * tk, (u + 1) * tk)
            cs.append(jnp.dot(jnp.concatenate([hi[:, sl], lo[:, sl]], axis=1), tri, preferred_element_type=F32))
        a = [None] * n
        for u in reversed(range(n)):
            a[u] = jnp.exp2(log_beta[:, u * tk:(u + 1) * tk] - (before + cs[u][:, :tk]))
            before = before + cs[u][:, tk:]
        a = jnp.concatenate(a, axis=1)
        if masked:
            a = jnp.where(strict, a, 0.0)
        acc = acc + jnp.dot(a.astype(BF16), v_ref[rows, :], preferred_element_type=F32)
        return before, acc

    row = qb * t + lax.broadcasted_iota(jnp.int32, (t, t), 0)
    carry = (jnp.zeros((t, tk), F32), jnp.zeros((t, HEAD_DIM), F32))
    carry = chunk(qb * per, per, True, carry)
    n_rest = qb * per
    n_tail = n_rest % unroll
    carry = lax.fori_loop(0, n_tail // per, lambda i, c: chunk(n_rest - (i + 1) * per, per, False, c), carry)
    carry = lax.fori_loop(0, n_rest // unroll,
                          lambda i, c: chunk(n_rest - n_tail - (i + 1) * unroll, unroll, False, c), carry)
    o_ref[...] = carry[1].astype(o_ref.dtype)


def _sb_tri(tk):
    j = jnp.arange(tk)
    half = jnp.concatenate([j[:, None] > j[None, :], jnp.ones((tk, tk), bool)], axis=1)
    return jnp.concatenate([half, half], axis=0).astype(BF16)


def _sb_attn(rest, tri, *, t, tk, heads, q_blk, k_blk, v_blk, unroll=4):
    s = rest.shape[0]
    return pl.pallas_call(
        functools.partial(_sb_kernel, t=t, tk=tk, unroll=unroll),
        out_shape=jax.ShapeDtypeStruct((s, heads * HEAD_DIM), BF16),
        grid=(heads, s // t),
        in_specs=[pl.BlockSpec((t, HEAD_DIM), lambda h, i: (i, q_blk + h)),
                  pl.BlockSpec((s, HEAD_DIM), lambda h, i: (0, k_blk + h)),
                  pl.BlockSpec((s, HEAD_DIM), lambda h, i: (0, v_blk + h)),
                  pl.BlockSpec((2 * tk, 2 * tk), lambda h, i: (0, 0))],
        out_specs=pl.BlockSpec((t, HEAD_DIM), lambda h, i: (i, h)),
        compiler_params=_cparams(2),
        name="sb_attn",
    )(rest, rest, rest, tri)


def _rel_bucket(n):
    max_exact = N_BUCKETS // 2
    nf = jnp.maximum(n, 1).astype(F32)
    large = max_exact + (jnp.log(nf / max_exact) / math.log(MAX_DISTANCE / max_exact)
                         * (N_BUCKETS - max_exact)).astype(jnp.int32)
    large = jnp.minimum(large, N_BUCKETS - 1)
    return jnp.where(n < max_exact, n, large)


def _bias_tables(rel_bias, t):
    assert t >= MAX_DISTANCE
    by_dist = (rel_bias[_rel_bucket(jnp.arange(2 * t))] - rel_bias[N_BUCKETS - 1]) * math.log2(math.e)
    r = jnp.arange(t)[:, None]
    c = jnp.arange(t)[None, :]
    tdiag = jnp.transpose(by_dist[jnp.maximum(r - c, 0)], (2, 0, 1))
    toff = jnp.transpose(by_dist[t + r - c], (2, 0, 1))
    return tdiag, toff


def kernel(x, positions, rel_bias, ffn1_norm, ffn1_gate, ffn1_up, ffn1_down, mix_norm, w_in, pool_w,
           pool_scale, q_norm, k_norm, w_out, ffn2_norm, ffn2_gate, ffn2_up, ffn2_down):
    del positions
    b, s, d = x.shape
    assert b == 1
    depth = w_in.shape[0]
    pool_width = pool_w.shape[1] * pool_w.shape[2]
    dsa_width = (w_in.shape[2] - pool_width - IDX_HEADS * IDX_DIM - IDX_DIM - IDX_HEADS) // 6
    sb_width = dsa_width
    dsa_heads = dsa_width // HEAD_DIM
    sb_heads = sb_width // HEAD_DIM
    topk = min(INDEX_TOPK, s // 4)
    t = _pick(s, 256)
    t_att = _pick(s, 512)
    tk_sb = 128
    t_sb = t_att

    o = np.cumsum([0, pool_width, dsa_width, dsa_width, dsa_width, IDX_HEADS * IDX_DIM, IDX_DIM, IDX_HEADS,
                   sb_width, sb_width, sb_width])
    small_pad = LANES - IDX_DIM - IDX_HEADS

    tdiag, toff = _bias_tables(rel_bias, t_att)
    tri = _sb_tri(tk_sb)

    def ffn(xin, norm, wg, wu, wd):
        xn = _rmsnorm(xin, norm)
        act = _matmul([xn], [wg.astype(BF16), wu.astype(BF16)], mode="swiglu", out_dtype=BF16,
                      tm=1024, tn=512, name="ffn_up")
        return _matmul([act], [wd.astype(BF16)], mode="residual", out_dtype=F32, tm=512, tn=512,
                       extra=xin, res_scale=0.5, name="ffn_down")

    h = x[0]
    for i in range(depth):
        h = ffn(h, ffn1_norm[i], ffn1_gate[i], ffn1_up[i], ffn1_down[i])

        hn = _rmsnorm(h, mix_norm[i])
        wi_l = w_in[i]
        w_pool = wi_l[:, o[0]:o[1]].astype(BF16)
        w_qk = wi_l[:, o[1]:o[3]].astype(BF16)
        w_rest = jnp.concatenate([wi_l[:, o[3]:o[5]], wi_l[:, o[7]:o[10]]], axis=1).astype(BF16)
        w_small = jnp.pad(wi_l[:, o[5]:o[7]], ((0, 0), (0, small_pad))).astype(BF16)
        gain = jnp.concatenate([jnp.tile(q_norm[i], dsa_heads), jnp.tile(k_norm[i], dsa_heads)])[None, :]

        u_pool = _matmul([hn], [w_pool], mode="cast", out_dtype=F32, tm=1024, tn=512, name="proj_pool")
        qk = _matmul([hn], [w_qk], mode="headnorm", out_dtype=BF16, tm=1024, tn=512, extra=gain,
                     name="proj_qk")
        rest = _matmul([hn], [w_rest], mode="cast", out_dtype=BF16, tm=1024, tn=512, name="proj_rest")
        small = _matmul([hn], [w_small], mode="cast", out_dtype=F32, tm=1024, tn=LANES, name="proj_small")

        y_pool = _pool_mixer(u_pool, pool_w[i], pool_scale[i])

        qi_hm = jnp.transpose(rest[:, dsa_width:dsa_width + IDX_HEADS * IDX_DIM].reshape(s, IDX_HEADS, IDX_DIM),
                              (1, 0, 2))
        kit = jnp.transpose(small[:, :IDX_DIM].astype(BF16).reshape(s // t, t, IDX_DIM), (0, 2, 1))
        wi = small[:, IDX_DIM:IDX_DIM + IDX_HEADS]
        mask = _dsa_index(qi_hm, kit, wi, t=t, topk=topk)
        y_dsa = _dsa_attn(qk, rest, mask, tdiag, toff, t=t_att, heads=dsa_heads)

        sb0 = (dsa_width + IDX_HEADS * IDX_DIM) // HEAD_DIM
        y_sb = _sb_attn(rest, tri, t=t_sb, tk=tk_sb, heads=sb_heads, q_blk=sb0, k_blk=sb0 + sb_heads,
                        v_blk=sb0 + 2 * sb_heads)

        wo = w_out[i].astype(BF16)
        p1, p2 = pool_width, pool_width + dsa_width
        h = _matmul([y_pool, y_dsa, y_sb], [wo[:p1], wo[p1:p2], wo[p2:]], mode="residual", out_dtype=F32,
                    tm=1024, tn=512, extra=h, name="mix_out")

        h = ffn(h, ffn2_norm[i], ffn2_gate[i], ffn2_up[i], ffn2_down[i])
    return h[None]
```

```python
import functools
import math

import jax
import jax.numpy as jnp
import numpy as np
from jax import lax
from jax.experimental import pallas as pl
from jax.experimental.pallas import tpu as pltpu

HEAD_DIM = 128
POOL_WINDOWS = (2, 4, 8, 16)
IDX_HEADS = 16
IDX_DIM = 64
INDEX_TOPK = 256
N_BUCKETS = 32
MAX_DISTANCE = 128
EPS = 1e-6

LANES = 128
SUBLANES = 8
POOL_HISTORY = 16
NEG = -1e30
INT_MIN = -(2 ** 31)
SB_EXIT_BITS = 160.0
VMEM_LIMIT = 56 * 1024 * 1024

F32 = jnp.float32
BF16 = jnp.bfloat16


def _cparams(n_grid):
    return pltpu.CompilerParams(dimension_semantics=("arbitrary",) * n_grid,
                                vmem_limit_bytes=VMEM_LIMIT)


def _pick(n, pref):
    t = min(pref, n)
    while n % t:
        t //= 2
    return t


def _rmsnorm_kernel(x_ref, g_ref, o_ref):
    x = x_ref[...]
    ms = jnp.mean(x * x, axis=-1, keepdims=True)
    o_ref[...] = (x * lax.rsqrt(ms + EPS) * g_ref[...]).astype(o_ref.dtype)


def _rmsnorm(x, g):
    s, d = x.shape
    tm = _pick(s, 256)
    return pl.pallas_call(
        _rmsnorm_kernel,
        out_shape=jax.ShapeDtypeStruct((s, d), BF16),
        grid=(s // tm,),
        in_specs=[pl.BlockSpec((tm, d), lambda i: (i, 0)),
                  pl.BlockSpec((1, d), lambda i: (0, 0))],
        out_specs=pl.BlockSpec((tm, d), lambda i: (i, 0)),
        compiler_params=_cparams(1),
        name="rmsnorm",
    )(x, g.reshape(1, d))


def _mm_kernel(a_ref, *refs, mode, res_scale):
    o_ref = refs[-1]
    a = a_ref[...]
    if mode == "swiglu":
        g = jnp.dot(a, refs[0][...], preferred_element_type=F32)
        u = jnp.dot(a, refs[1][...], preferred_element_type=F32)
        o_ref[...] = (g / (1.0 + jnp.exp(-g)) * u).astype(o_ref.dtype)
        return
    acc = jnp.dot(a, refs[0][...], preferred_element_type=F32)
    extra = refs[1:-1]
    if mode == "cast":
        o_ref[...] = acc.astype(o_ref.dtype)
    elif mode == "residual":
        o_ref[...] = extra[0][...] + res_scale * acc
    elif mode == "headnorm":
        gain = extra[0][...]
        for c in range(acc.shape[1] // HEAD_DIM):
            sl = slice(c * HEAD_DIM, (c + 1) * HEAD_DIM)
            y = acc[:, sl]
            ms = jnp.mean(y * y, axis=-1, keepdims=True)
            o_ref[:, sl] = (y * lax.rsqrt(ms + EPS) * gain[:, sl]).astype(o_ref.dtype)
    else:
        raise ValueError(mode)


def _matmul(a, ws, layer, *, mode, out_dtype, tm, tn, col0=0, n=None, extra=None, res_scale=1.0, name):
    m, k = a.shape
    n = ws[0].shape[2] if n is None else n
    tm, tn = _pick(m, tm), _pick(n, tn)
    assert col0 % tn == 0
    in_specs = [pl.BlockSpec((tm, k), lambda i, j: (i, 0))]
    in_specs += [pl.BlockSpec((None, k, tn), lambda i, j: (layer, 0, col0 // tn + j)) for _ in ws]
    operands = [a] + list(ws)
    if mode == "residual":
        in_specs.append(pl.BlockSpec((tm, tn), lambda i, j: (i, j)))
        operands.append(extra)
    elif mode == "headnorm":
        in_specs.append(pl.BlockSpec((1, tn), lambda i, j: (0, j)))
        operands.append(extra)
    return pl.pallas_call(
        functools.partial(_mm_kernel, mode=mode, res_scale=res_scale),
        out_shape=jax.ShapeDtypeStruct((m, n), out_dtype),
        grid=(m // tm, n // tn),
        in_specs=in_specs,
        out_specs=pl.BlockSpec((tm, tn), lambda i, j: (i, j)),
        compiler_params=_cparams(2),
        name=name,
    )(*operands)


def _pool_kernel(cur_ref, prev_ref, w_ref, scale_ref, o_ref, *, tm, gd):
    i = pl.program_id(0)
    cur = cur_ref[...]
    prev = jnp.where(i > 0, prev_ref[...], 0.0)
    ext = jnp.concatenate([prev, cur], axis=0)
    t = i * tm + lax.broadcasted_iota(jnp.int32, (tm, 1), 0)
    sums = {1: ext}
    w = 1
    while w < max(POOL_WINDOWS):
        sums[2 * w] = sums[w] + pltpu.roll(sums[w], w, axis=0)
        w *= 2
    for g, win in enumerate(POOL_WINDOWS):
        sl = slice(g * gd, (g + 1) * gd)
        cnt = jnp.minimum(t + 1, win).astype(F32)
        d = sums[win][POOL_HISTORY:, sl] / cnt - cur[:, sl]
        y = jnp.dot(d.astype(BF16), w_ref[g], preferred_element_type=F32)
        o_ref[:, sl] = (y * scale_ref[:, sl]).astype(o_ref.dtype)


def _pool_mixer(u_pool, pool_w, pool_scale):
    s, pw = u_pool.shape
    groups, gd, _ = pool_w.shape
    tm = _pick(s, 512)
    hist_blocks = tm // POOL_HISTORY
    return pl.pallas_call(
        functools.partial(_pool_kernel, tm=tm, gd=gd),
        out_shape=jax.ShapeDtypeStruct((s, pw), BF16),
        grid=(s // tm,),
        in_specs=[pl.BlockSpec((tm, pw), lambda i: (i, 0)),
                  pl.BlockSpec((POOL_HISTORY, pw), lambda i: (jnp.maximum(i * hist_blocks - 1, 0), 0)),
                  pl.BlockSpec((groups, gd, gd), lambda i: (0, 0, 0)),
                  pl.BlockSpec((1, pw), lambda i: (0, 0))],
        out_specs=pl.BlockSpec((tm, pw), lambda i: (i, 0)),
        compiler_params=_cparams(1),
        name="pool_mixer",
    )(u_pool, u_pool, pool_w.astype(BF16), pool_scale.reshape(1, pw))


def _dsa_index_kernel(qi_ref, kit_ref, w_ref, o_ref, keys_ref, keyst_ref, wb_ref, *, t, topk):
    qb = pl.program_id(0)
    nkb = o_ref.shape[0]
    w = w_ref[...] * (IDX_DIM ** -0.5 * IDX_HEADS ** -0.5)
    for h in range(IDX_HEADS):
        wb_ref[h] = jnp.broadcast_to(w[:, h:h + 1], (t, t))
    row = qb * t + lax.broadcasted_iota(jnp.int32, (t, t), 0)
    col0 = lax.broadcasted_iota(jnp.int32, (t, t), 1)

    def score_tile(kb, carry):
        kt = kit_ref[kb]
        acc = jnp.zeros((t, t), F32)
        for h in range(IDX_HEADS):
            sc = jnp.dot(qi_ref[h], kt, preferred_element_type=F32)
            acc = acc + jnp.maximum(sc, 0.0) * wb_ref[h]
        bits = pltpu.bitcast(acc, jnp.int32)
        key = jnp.where(bits < 0, bits ^ 0x7FFFFFFF, bits)
        key = jnp.where(kb * t + col0 <= row, key, INT_MIN)
        keys_ref[kb] = key
        keyst_ref[kb] = key.T
        return carry

    lax.fori_loop(0, qb + 1, score_tile, 0)

    n_acc = 4

    def bit_step(it, thr):
        cand = thr + jnp.left_shift(jnp.int32(1), 31 - it)
        cand_b = jnp.broadcast_to(cand, (SUBLANES, t))

        def count_tile(kb, cs):
            cs = list(cs)
            for g in range(t // SUBLANES):
                kk = keyst_ref[kb, g * SUBLANES:(g + 1) * SUBLANES, :]
                cs[g % n_acc] = cs[g % n_acc] + jnp.where(kk >= cand_b, 1, 0)
            return tuple(cs)

        cs = lax.fori_loop(0, qb + 1, count_tile,
                           tuple(jnp.zeros((SUBLANES, t), jnp.int32) for _ in range(n_acc)))
        c = (cs[0] + cs[1]) + (cs[2] + cs[3])
        cnt = jnp.sum(c.astype(F32), axis=0, keepdims=True)
        return jnp.where(cnt >= float(topk), cand, thr)

    thr = lax.fori_loop(0, 32, bit_step, jnp.full((1, t), INT_MIN, jnp.int32))
    thr = jnp.maximum(thr, INT_MIN + 1)
    thr_b = jnp.broadcast_to(thr, (LANES, t)).T
    thr_b = jnp.concatenate([thr_b] * (t // LANES), axis=1)

    def write_tile(kb, carry):
        o_ref[kb] = jnp.where(keys_ref[kb] >= thr_b, 0.0, NEG).astype(o_ref.dtype)
        return carry

    def fill_tile(kb, carry):
        o_ref[kb] = jnp.full((t, t), NEG, o_ref.dtype)
        return carry

    lax.fori_loop(0, qb + 1, write_tile, 0)
    lax.fori_loop(qb + 1, nkb, fill_tile, 0)


def _dsa_index(qi_hm, kit, wi, *, t, topk):
    heads, s, idim = qi_hm.shape
    nkb = s // t
    return pl.pallas_call(
        functools.partial(_dsa_index_kernel, t=t, topk=topk),
        out_shape=jax.ShapeDtypeStruct((nkb, s, t), BF16),
        grid=(nkb,),
        in_specs=[pl.BlockSpec((heads, t, idim), lambda i: (0, i, 0)),
                  pl.BlockSpec((nkb, idim, t), lambda i: (0, 0, 0)),
                  pl.BlockSpec((t, heads), lambda i: (i, 0))],
        out_specs=pl.BlockSpec((nkb, t, t), lambda i: (0, i, 0)),
        scratch_shapes=[pltpu.VMEM((nkb, t, t), jnp.int32),
                        pltpu.VMEM((nkb, t, t), jnp.int32),
                        pltpu.VMEM((heads, t, t), F32)],
        compiler_params=_cparams(1),
        name="dsa_index",
    )(qi_hm, kit, wi)


def _dsa_attn_kernel(q_ref, k_ref, v_ref, mask_ref, near_ref, o_ref, *, t, tm):
    qb = pl.program_id(1)
    q = q_ref[...]
    scale2 = HEAD_DIM ** -0.5 * math.log2(math.e)
    per = t // tm
    nb = t // MAX_DISTANCE

    def add_near(s, which):
        out_rows = []
        for i in range(nb):
            blocks = []
            for j in range(nb):
                blk = s[i * MAX_DISTANCE:(i + 1) * MAX_DISTANCE, j * MAX_DISTANCE:(j + 1) * MAX_DISTANCE]
                if which == "diag" and i == j:
                    blk = blk + near_ref[0, 0]
                elif (which == "diag" and i == j + 1) or (which == "before" and i == 0 and j == nb - 1):
                    blk = blk + near_ref[0, 1]
                blocks.append(blk)
            out_rows.append(jnp.concatenate(blocks, axis=1))
        return jnp.concatenate(out_rows, axis=0)

    def chunk(kc, which, carry):
        m, l, acc = carry
        rows = pl.ds(pl.multiple_of(kc * t, t), t)
        s = lax.dot_general(q, k_ref[rows, :], (((1,), (1,)), ((), ())), preferred_element_type=F32) * scale2
        s = s + jnp.concatenate([mask_ref[kc * per + j] for j in range(per)], axis=1).astype(F32)
        if which is not None:
            s = add_near(s, which)
        m_new = jnp.maximum(m, jnp.max(s, axis=1, keepdims=True))
        alpha = jnp.exp2(m - m_new)
        p = jnp.exp2(s - m_new)
        l = alpha * l + jnp.sum(p, axis=1, keepdims=True)
        acc = alpha * acc + jnp.dot(p.astype(BF16), v_ref[rows, :], preferred_element_type=F32)
        return m_new, l, acc

    carry = (jnp.full((t, 1), NEG, F32), jnp.zeros((t, 1), F32), jnp.zeros((t, HEAD_DIM), F32))
    carry = lax.fori_loop(0, jnp.maximum(qb - 1, 0), lambda kc, c: chunk(kc, None, c), carry)
    carry = lax.fori_loop(jnp.maximum(qb - 1, 0), qb, lambda kc, c: chunk(kc, "before", c), carry)
    _, l, acc = chunk(qb, "diag", carry)
    o_ref[...] = (acc / l).astype(o_ref.dtype)


def _dsa_attn(qk, rest, mask, near, *, t, heads):
    s = qk.shape[0]
    nkb, _, tm = mask.shape
    return pl.pallas_call(
        functools.partial(_dsa_attn_kernel, t=t, tm=tm),
        out_shape=jax.ShapeDtypeStruct((s, heads * HEAD_DIM), BF16),
        grid=(heads, s // t),
        in_specs=[pl.BlockSpec((t, HEAD_DIM), lambda h, i: (i, h)),
                  pl.BlockSpec((s, HEAD_DIM), lambda h, i: (0, heads + h)),
                  pl.BlockSpec((s, HEAD_DIM), lambda h, i: (0, h)),
                  pl.BlockSpec((nkb, t, tm), lambda h, i: (0, i, 0)),
                  pl.BlockSpec((1, 2, MAX_DISTANCE, MAX_DISTANCE), lambda h, i: (h, 0, 0, 0))],
        out_specs=pl.BlockSpec((t, HEAD_DIM), lambda h, i: (i, h)),
        compiler_params=_cparams(2),
        name="dsa_attn",
    )(qk, qk, rest, mask, near)


def _sb_kernel(q_ref, k_ref, v_ref, tri_ref, o_ref, *, t, tk):
    qb = pl.program_id(1)
    q = q_ref[...]
    tri = tri_ref[...]
    scale2 = HEAD_DIM ** -0.5 * math.log2(math.e)
    per = t // tk
    small = max(per // 2, 1)

    def chunk(kb0, n, masked, carry):
        before, acc = carry
        width = n * tk
        rows = pl.ds(pl.multiple_of(kb0 * tk, tk), width)
        z = lax.dot_general(q, k_ref[rows, :], (((1,), (1,)), ((), ())), preferred_element_type=F32) * scale2
        lo_z, hi_z = jnp.minimum(z, 0.0), jnp.maximum(z, 0.0)
        lg = jnp.log2(1.0 + jnp.exp2(lo_z - hi_z))
        sp = hi_z + lg
        log_beta = lo_z - lg
        if masked:
            strict = kb0 * tk + lax.broadcasted_iota(jnp.int32, (t, width), 1) < row
            sp = jnp.where(strict, sp, 0.0)
        hi = sp.astype(BF16)
        lo = (sp - hi.astype(F32)).astype(BF16)
        cs = []
        for u in range(n):
            sl = slice(u * tk, (u + 1) * tk)
            cs.append(jnp.dot(jnp.concatenate([hi[:, sl], lo[:, sl]], axis=1), tri, preferred_element_type=F32))
        a = [None] * n
        for u in reversed(range(n)):
            a[u] = jnp.exp2(log_beta[:, u * tk:(u + 1) * tk] - (before + cs[u][:, :tk]))
            before = before + cs[u][:, tk:]
        a = jnp.concatenate(a, axis=1)
        if masked:
            a = jnp.where(strict, a, 0.0)
        acc = acc + jnp.dot(a.astype(BF16), v_ref[rows, :], preferred_element_type=F32)
        return before, acc

    def walk(n_iters, kb0_of, n, state):
        def cond(c):
            return jnp.logical_and(c[0] < n_iters, c[1])

        def body(c):
            before, acc = chunk(kb0_of(c[0]), n, False, c[2:])
            return c[0] + 1, jnp.min(before) < SB_EXIT_BITS, before, acc

        return lax.while_loop(cond, body, (jnp.int32(0),) + state[1:])

    row = qb * t + lax.broadcasted_iota(jnp.int32, (t, t), 0)
    carry = (jnp.zeros((t, tk), F32), jnp.zeros((t, HEAD_DIM), F32))
    before, acc = chunk(qb * per, per, True, carry)
    state = (jnp.int32(0), jnp.min(before) < SB_EXIT_BITS, before, acc)
    n_rest = qb * per
    n_small = jnp.minimum(n_rest, per) // small
    state = walk(n_small, lambda i: n_rest - (i + 1) * small, small, state)
    state = walk(qb - 1, lambda i: n_rest - per - (i + 1) * per, per, state)
    o_ref[...] = state[3].astype(o_ref.dtype)


def _sb_tri(tk):
    j = jnp.arange(tk)
    half = jnp.concatenate([j[:, None] > j[None, :], jnp.ones((tk, tk), bool)], axis=1)
    return jnp.concatenate([half, half], axis=0).astype(BF16)


def _sb_attn(rest, tri, *, t, tk, heads, q_blk, k_blk, v_blk):
    s = rest.shape[0]
    return pl.pallas_call(
        functools.partial(_sb_kernel, t=t, tk=tk),
        out_shape=jax.ShapeDtypeStruct((s, heads * HEAD_DIM), BF16),
        grid=(heads, s // t),
        in_specs=[pl.BlockSpec((t, HEAD_DIM), lambda h, i: (i, q_blk + h)),
                  pl.BlockSpec((s, HEAD_DIM), lambda h, i: (0, k_blk + h)),
                  pl.BlockSpec((s, HEAD_DIM), lambda h, i: (0, v_blk + h)),
                  pl.BlockSpec((2 * tk, 2 * tk), lambda h, i: (0, 0))],
        out_specs=pl.BlockSpec((t, HEAD_DIM), lambda h, i: (i, h)),
        compiler_params=_cparams(2),
        name="sb_attn",
    )(rest, rest, rest, tri)


def _rel_bucket(n):
    max_exact = N_BUCKETS // 2
    nf = jnp.maximum(n, 1).astype(F32)
    large = max_exact + (jnp.log(nf / max_exact) / math.log(MAX_DISTANCE / max_exact)
                         * (N_BUCKETS - max_exact)).astype(jnp.int32)
    large = jnp.minimum(large, N_BUCKETS - 1)
    return jnp.where(n < max_exact, n, large)


def _bias_tables(rel_bias):
    b = MAX_DISTANCE
    by_dist = (rel_bias[_rel_bucket(jnp.arange(2 * b))] - rel_bias[N_BUCKETS - 1]) * math.log2(math.e)
    r = jnp.arange(b)[:, None]
    c = jnp.arange(b)[None, :]
    near = jnp.stack([by_dist[jnp.maximum(r - c, 0)], by_dist[b + r - c]])
    return jnp.transpose(near, (3, 0, 1, 2))


def kernel(x, positions, rel_bias, ffn1_norm, ffn1_gate, ffn1_up, ffn1_down, mix_norm, w_in, pool_w,
           pool_scale, q_norm, k_norm, w_out, ffn2_norm, ffn2_gate, ffn2_up, ffn2_down):
    del positions
    b, s, d = x.shape
    assert b == 1
    depth = w_in.shape[0]
    pool_width = pool_w.shape[1] * pool_w.shape[2]
    dsa_width = (w_in.shape[2] - pool_width - IDX_HEADS * IDX_DIM - IDX_DIM - IDX_HEADS) // 6
    sb_width = dsa_width
    dsa_heads = dsa_width // HEAD_DIM
    sb_heads = sb_width // HEAD_DIM
    topk = min(INDEX_TOPK, s // 4)
    t = _pick(s, 256)
    t_att = _pick(s, 512)
    tk_sb = 128
    t_sb = t_att

    o = np.cumsum([0, pool_width, dsa_width, dsa_width, dsa_width, IDX_HEADS * IDX_DIM, IDX_DIM, IDX_HEADS,
                   sb_width, sb_width, sb_width])
    small_pad = LANES - IDX_DIM - IDX_HEADS
    w_in_r = jnp.concatenate([w_in[:, :, o[0]:o[5]], w_in[:, :, o[7]:o[10]], w_in[:, :, o[5]:o[7]],
                              jnp.zeros(w_in.shape[:2] + (small_pad,), w_in.dtype)], axis=2).astype(BF16)
    c_qk = pool_width
    c_rest = c_qk + 2 * dsa_width
    c_small = c_rest + dsa_width + IDX_HEADS * IDX_DIM + 3 * sb_width
    n_rest = c_small - c_rest
    w_out_b = w_out.astype(BF16)
    ffn1 = (ffn1_norm, ffn1_gate.astype(BF16), ffn1_up.astype(BF16), ffn1_down.astype(BF16))
    ffn2 = (ffn2_norm, ffn2_gate.astype(BF16), ffn2_up.astype(BF16), ffn2_down.astype(BF16))

    near = _bias_tables(rel_bias)
    tri = _sb_tri(tk_sb)

    def ffn(xin, weights, layer):
        norm, wg, wu, wd = weights
        xn = _rmsnorm(xin, norm[layer])
        act = _matmul(xn, [wg, wu], layer, mode="swiglu", out_dtype=BF16, tm=1024, tn=512, name="ffn_up")
        return _matmul(act, [wd], layer, mode="residual", out_dtype=F32, tm=512, tn=512, extra=xin,
                       res_scale=0.5, name="ffn_down")

    h = x[0]
    for i in range(depth):
        h = ffn(h, ffn1, i)

        hn = _rmsnorm(h, mix_norm[i])
        gain = jnp.concatenate([jnp.tile(q_norm[i], dsa_heads), jnp.tile(k_norm[i], dsa_heads)])[None, :]
        u_pool = _matmul(hn, [w_in_r], i, mode="cast", out_dtype=F32, tm=1024, tn=512, col0=0, n=pool_width,
                         name="proj_pool")
        qk = _matmul(hn, [w_in_r], i, mode="headnorm", out_dtype=BF16, tm=1024, tn=512, col0=c_qk,
                     n=2 * dsa_width, extra=gain, name="proj_qk")
        rest = _matmul(hn, [w_in_r], i, mode="cast", out_dtype=BF16, tm=1024, tn=512, col0=c_rest, n=n_rest,
                       name="proj_rest")
        small = _matmul(hn, [w_in_r], i, mode="cast", out_dtype=F32, tm=1024, tn=LANES, col0=c_small, n=LANES,
                        name="proj_small")

        y_pool = _pool_mixer(u_pool, pool_w[i], pool_scale[i])

        qi_hm = jnp.transpose(rest[:, dsa_width:dsa_width + IDX_HEADS * IDX_DIM].reshape(s, IDX_HEADS, IDX_DIM),
                              (1, 0, 2))
        kit = jnp.transpose(small[:, :IDX_DIM].astype(BF16).reshape(s // t, t, IDX_DIM), (0, 2, 1))
        wi = small[:, IDX_DIM:IDX_DIM + IDX_HEADS]
        mask = _dsa_index(qi_hm, kit, wi, t=t, topk=topk)
        y_dsa = _dsa_attn(qk, rest, mask, near, t=t_att, heads=dsa_heads)

        sb0 = (dsa_width + IDX_HEADS * IDX_DIM) // HEAD_DIM
        y_sb = _sb_attn(rest, tri, t=t_sb, tk=tk_sb, heads=sb_heads, q_blk=sb0, k_blk=sb0 + sb_heads,
                        v_blk=sb0 + 2 * sb_heads)

        y = jnp.concatenate([y_pool, y_dsa, y_sb], axis=1)
        h = _matmul(y, [w_out_b], i, mode="residual", out_dtype=F32, tm=1024, tn=512, extra=h, name="mix_out")

        h = ffn(h, ffn2, i)
    return h[None]
```

```python
import functools
import math

import jax
import jax.numpy as jnp
import numpy as np
from jax import lax
from jax.experimental import pallas as pl
from jax.experimental.pallas import tpu as pltpu

HEAD_DIM = 128
POOL_WINDOWS = (2, 4, 8, 16)
IDX_HEADS = 16
IDX_DIM = 64
INDEX_TOPK = 256
N_BUCKETS = 32
MAX_DISTANCE = 128
EPS = 1e-6

LANES = 128
SUBLANES = 8
POOL_HISTORY = 16
NEG = -1e30
INT_MIN = -(2 ** 31)
DSA_LOGIT_SCALE2 = HEAD_DIM ** -0.5 * math.log2(math.e)
SB_EXIT_BITS = 160.0
VMEM_LIMIT = 56 * 1024 * 1024

F32 = jnp.float32
BF16 = jnp.bfloat16


def _cparams(n_grid):
    return pltpu.CompilerParams(dimension_semantics=("arbitrary",) * n_grid,
                                vmem_limit_bytes=VMEM_LIMIT)


def _pick(n, pref):
    t = min(pref, n)
    while n % t:
        t //= 2
    return t


def _rmsnorm_kernel(x_ref, g_ref, o_ref):
    x = x_ref[...]
    ms = jnp.mean(x * x, axis=-1, keepdims=True)
    o_ref[...] = (x * lax.rsqrt(ms + EPS) * g_ref[...]).astype(o_ref.dtype)


def _rmsnorm(x, g):
    s, d = x.shape
    tm = _pick(s, 256)
    return pl.pallas_call(
        _rmsnorm_kernel,
        out_shape=jax.ShapeDtypeStruct((s, d), BF16),
        grid=(s // tm,),
        in_specs=[pl.BlockSpec((tm, d), lambda i: (i, 0)),
                  pl.BlockSpec((1, d), lambda i: (0, 0))],
        out_specs=pl.BlockSpec((tm, d), lambda i: (i, 0)),
        compiler_params=_cparams(1),
        name="rmsnorm",
    )(x, g.reshape(1, d))


def _mm_kernel(a_ref, *refs, mode, res_scale):
    o_ref = refs[-1]
    a = a_ref[...]
    if mode == "swiglu":
        g = jnp.dot(a, refs[0][...], preferred_element_type=F32)
        u = jnp.dot(a, refs[1][...], preferred_element_type=F32)
        o_ref[...] = (g / (1.0 + jnp.exp(-g)) * u).astype(o_ref.dtype)
        return
    acc = jnp.dot(a, refs[0][...], preferred_element_type=F32)
    extra = refs[1:-1]
    if mode == "cast":
        o_ref[...] = acc.astype(o_ref.dtype)
    elif mode == "residual":
        o_ref[...] = extra[0][...] + res_scale * acc
    elif mode == "headnorm":
        gain = extra[0][...]
        for c in range(acc.shape[1] // HEAD_DIM):
            sl = slice(c * HEAD_DIM, (c + 1) * HEAD_DIM)
            y = acc[:, sl]
            ms = jnp.mean(y * y, axis=-1, keepdims=True)
            o_ref[:, sl] = (y * lax.rsqrt(ms + EPS) * gain[:, sl]).astype(o_ref.dtype)
    else:
        raise ValueError(mode)


def _matmul(a, ws, layer, *, mode, out_dtype, tm, tn, col0=0, n=None, extra=None, res_scale=1.0, name):
    m, k = a.shape
    n = ws[0].shape[2] if n is None else n
    tm, tn = _pick(m, tm), _pick(n, tn)
    assert col0 % tn == 0
    in_specs = [pl.BlockSpec((tm, k), lambda i, j: (i, 0))]
    in_specs += [pl.BlockSpec((None, k, tn), lambda i, j: (layer, 0, col0 // tn + j)) for _ in ws]
    operands = [a] + list(ws)
    if mode == "residual":
        in_specs.append(pl.BlockSpec((tm, tn), lambda i, j: (i, j)))
        operands.append(extra)
    elif mode == "headnorm":
        in_specs.append(pl.BlockSpec((1, tn), lambda i, j: (0, j)))
        operands.append(extra)
    return pl.pallas_call(
        functools.partial(_mm_kernel, mode=mode, res_scale=res_scale),
        out_shape=jax.ShapeDtypeStruct((m, n), out_dtype),
        grid=(m // tm, n // tn),
        in_specs=in_specs,
        out_specs=pl.BlockSpec((tm, tn), lambda i, j: (i, j)),
        compiler_params=_cparams(2),
        name=name,
    )(*operands)


def _pool_kernel(cur_ref, prev_ref, w_ref, scale_ref, o_ref, *, tm, gd):
    i = pl.program_id(0)
    cur = cur_ref[...]
    prev = jnp.where(i > 0, prev_ref[...], 0.0)
    ext = jnp.concatenate([prev, cur], axis=0)
    t = i * tm + lax.broadcasted_iota(jnp.int32, (tm, 1), 0)
    sums = {1: ext}
    w = 1
    while w < max(POOL_WINDOWS):
        sums[2 * w] = sums[w] + pltpu.roll(sums[w], w, axis=0)
        w *= 2
    for g, win in enumerate(POOL_WINDOWS):
        sl = slice(g * gd, (g + 1) * gd)
        cnt = jnp.minimum(t + 1, win).astype(F32)
        d = sums[win][POOL_HISTORY:, sl] / cnt - cur[:, sl]
        y = jnp.dot(d.astype(BF16), w_ref[g], preferred_element_type=F32)
        o_ref[:, sl] = (y * scale_ref[:, sl]).astype(o_ref.dtype)


def _pool_mixer(u_pool, pool_w, pool_scale):
    s, pw = u_pool.shape
    groups, gd, _ = pool_w.shape
    tm = _pick(s, 512)
    hist_blocks = tm // POOL_HISTORY
    return pl.pallas_call(
        functools.partial(_pool_kernel, tm=tm, gd=gd),
        out_shape=jax.ShapeDtypeStruct((s, pw), BF16),
        grid=(s // tm,),
        in_specs=[pl.BlockSpec((tm, pw), lambda i: (i, 0)),
                  pl.BlockSpec((POOL_HISTORY, pw), lambda i: (jnp.maximum(i * hist_blocks - 1, 0), 0)),
                  pl.BlockSpec((groups, gd, gd), lambda i: (0, 0, 0)),
                  pl.BlockSpec((1, pw), lambda i: (0, 0))],
        out_specs=pl.BlockSpec((tm, pw), lambda i: (i, 0)),
        compiler_params=_cparams(1),
        name="pool_mixer",
    )(u_pool, u_pool, pool_w.astype(BF16), pool_scale.reshape(1, pw))


def _dsa_index_kernel(qi_ref, kit_ref, w_ref, o_ref, keys_ref, keyst_ref, hit_ref, wb_ref, *, t, topk):
    qb = pl.program_id(0)
    nkb = o_ref.shape[0]
    w = w_ref[...] * (IDX_DIM ** -0.5 * IDX_HEADS ** -0.5)
    for h in range(IDX_HEADS):
        wb_ref[h] = jnp.broadcast_to(w[:, h:h + 1], (t, t))
    row = qb * t + lax.broadcasted_iota(jnp.int32, (t, t), 0)
    col0 = lax.broadcasted_iota(jnp.int32, (t, t), 1)

    def score_tile(kb, carry):
        kt = kit_ref[kb]
        acc = jnp.zeros((t, t), F32)
        for h in range(IDX_HEADS):
            sc = jnp.dot(qi_ref[h], kt, preferred_element_type=F32)
            acc = acc + jnp.maximum(sc, 0.0) * wb_ref[h]
        acc = jnp.where(acc == 0.0, 0.0, acc)
        bits = pltpu.bitcast(acc, jnp.int32)
        key = jnp.where(bits < 0, bits ^ 0x7FFFFFFF, bits)
        key = jnp.where(kb * t + col0 <= row, key, INT_MIN)
        keys_ref[kb] = key
        key_t = key.T
        keyst_ref[kb] = key_t
        bits_t = jnp.where(key_t < 0, key_t ^ 0x7FFFFFFF, key_t) & -65536
        hit_ref[kb] = pltpu.bitcast(bits_t, F32).astype(BF16)
        return carry

    lax.fori_loop(0, qb + 1, score_tile, 0)

    n_acc = 4
    pack = 2 * SUBLANES

    def hi_step(it, thr16):
        cand16 = thr16 + jnp.left_shift(jnp.int32(1), 15 - it)
        cand_bits = jnp.left_shift(jnp.where(cand16 < 0, cand16 ^ 0x7FFF, cand16), 16)
        cand_b = jnp.broadcast_to(pltpu.bitcast(cand_bits, F32).astype(BF16), (pack, t))

        def count_tile(kb, cs):
            cs = list(cs)
            for g in range(t // pack):
                xh = hit_ref[kb, g * pack:(g + 1) * pack, :]
                cs[g % n_acc] = cs[g % n_acc] + jnp.where(xh >= cand_b, jnp.ones((), BF16), jnp.zeros((), BF16))
            return tuple(cs)

        cs = lax.fori_loop(0, qb + 1, count_tile, tuple(jnp.zeros((pack, t), BF16) for _ in range(n_acc)))
        c = (cs[0].astype(F32) + cs[1].astype(F32)) + (cs[2].astype(F32) + cs[3].astype(F32))
        cnt = jnp.sum(c, axis=0, keepdims=True)
        return jnp.where(cnt >= float(topk), cand16, thr16)

    def lo_step(it, thr):
        cand = thr + jnp.left_shift(jnp.int32(1), 15 - it)
        cand_b = jnp.broadcast_to(cand, (SUBLANES, t))

        def count_tile(kb, cs):
            cs = list(cs)
            for g in range(t // SUBLANES):
                kk = keyst_ref[kb, g * SUBLANES:(g + 1) * SUBLANES, :]
                cs[g % n_acc] = cs[g % n_acc] + jnp.where(kk >= cand_b, 1, 0)
            return tuple(cs)

        cs = lax.fori_loop(0, qb + 1, count_tile,
                           tuple(jnp.zeros((SUBLANES, t), jnp.int32) for _ in range(n_acc)))
        c = (cs[0] + cs[1]) + (cs[2] + cs[3])
        cnt = jnp.sum(c.astype(F32), axis=0, keepdims=True)
        return jnp.where(cnt >= float(topk), cand, thr)

    assert nkb * t <= 256 * pack * n_acc
    thr16 = lax.fori_loop(0, 16, hi_step, jnp.full((1, t), -(2 ** 15), jnp.int32))
    thr = lax.fori_loop(0, 16, lo_step, jnp.left_shift(thr16, 16))
    thr = jnp.maximum(thr, INT_MIN + 1)
    thr_b = jnp.broadcast_to(thr, (LANES, t)).T
    thr_b = jnp.concatenate([thr_b] * (t // LANES), axis=1)

    def write_tile(kb, carry):
        o_ref[kb] = jnp.where(keys_ref[kb] >= thr_b, 0.0, NEG).astype(o_ref.dtype)
        return carry

    def fill_tile(kb, carry):
        o_ref[kb] = jnp.full((t, t), NEG, o_ref.dtype)
        return carry

    lax.fori_loop(0, qb + 1, write_tile, 0)
    lax.fori_loop(qb + 1, nkb, fill_tile, 0)


def _dsa_index(qi_hm, kit, wi, *, t, topk):
    heads, s, idim = qi_hm.shape
    nkb = s // t
    return pl.pallas_call(
        functools.partial(_dsa_index_kernel, t=t, topk=topk),
        out_shape=jax.ShapeDtypeStruct((nkb, s, t), BF16),
        grid=(nkb,),
        in_specs=[pl.BlockSpec((heads, t, idim), lambda i: (0, i, 0)),
                  pl.BlockSpec((nkb, idim, t), lambda i: (0, 0, 0)),
                  pl.BlockSpec((t, heads), lambda i: (i, 0))],
        out_specs=pl.BlockSpec((nkb, t, t), lambda i: (0, i, 0)),
        scratch_shapes=[pltpu.VMEM((nkb, t, t), jnp.int32),
                        pltpu.VMEM((nkb, t, t), jnp.int32),
                        pltpu.VMEM((nkb, t, t), BF16),
                        pltpu.VMEM((heads, t, t), F32)],
        compiler_params=_cparams(1),
        name="dsa_index",
    )(qi_hm, kit, wi)


def _dsa_attn_kernel(q_ref, k_ref, v_ref, mask_ref, near_ref, o_ref, s0_ref, s1_ref, *, t, tm):
    qb = pl.program_id(1)
    q = q_ref[...]
    per = t // tm
    nb = t // MAX_DISTANCE

    def add_near(s, which):
        out_rows = []
        for i in range(nb):
            blocks = []
            for j in range(nb):
                blk = s[i * MAX_DISTANCE:(i + 1) * MAX_DISTANCE, j * MAX_DISTANCE:(j + 1) * MAX_DISTANCE]
                if which == "diag" and i == j:
                    blk = blk + near_ref[0, 0]
                elif (which == "diag" and i == j + 1) or (which == "before" and i == 0 and j == nb - 1):
                    blk = blk + near_ref[0, 1]
                blocks.append(blk)
            out_rows.append(jnp.concatenate(blocks, axis=1))
        return jnp.concatenate(out_rows, axis=0)

    bufs = (s0_ref, s1_ref)

    def logits_into(kc, dst_ref):
        rows = pl.ds(pl.multiple_of(kc * t, t), t)
        dst_ref[...] = lax.dot_general(q, k_ref[rows, :], (((1,), (1,)), ((), ())), preferred_element_type=F32)

    def step(kc, which, parity, carry):
        if which != "diag":
            logits_into(kc + 1, bufs[1 - parity])
        m, l, acc = carry
        s = bufs[parity][...] + jnp.concatenate([mask_ref[kc * per + j] for j in range(per)],
                                                axis=1).astype(F32)
        if which != "far":
            s = add_near(s, which)
        m_new = jnp.maximum(m, jnp.max(s, axis=1, keepdims=True))
        alpha = jnp.exp2(m - m_new)
        p = jnp.exp2(s - m_new)
        l = alpha * l + jnp.sum(p, axis=1, keepdims=True)
        rows = pl.ds(pl.multiple_of(kc * t, t), t)
        acc = alpha * acc + jnp.dot(p.astype(BF16), v_ref[rows, :], preferred_element_type=F32)
        return m_new, l, acc

    def far_pair(i, carry):
        carry = step(2 * i, "far", 0, carry)
        return step(2 * i + 1, "far", 1, carry)

    def tail(kinds):
        def run(k0, carry):
            for n, which in enumerate(kinds):
                carry = step(k0 + n, which, n % 2, carry)
            return carry
        return run

    logits_into(0, s0_ref)
    n_far = jnp.maximum(qb - 1, 0)
    carry = (jnp.full((t, 1), NEG, F32), jnp.zeros((t, 1), F32), jnp.zeros((t, HEAD_DIM), F32))
    carry = lax.fori_loop(0, n_far // 2, far_pair, carry)
    k0 = 2 * (n_far // 2)
    case = jnp.where(qb == 0, 0, 1 + n_far % 2)
    _, l, acc = lax.switch(case, [tail(("diag",)), tail(("before", "diag")), tail(("far", "before", "diag"))],
                           k0, carry)
    o_ref[...] = (acc / l).astype(o_ref.dtype)


def _dsa_attn(qk, rest, mask, near, *, t, heads):
    s = qk.shape[0]
    nkb, _, tm = mask.shape
    return pl.pallas_call(
        functools.partial(_dsa_attn_kernel, t=t, tm=tm),
        out_shape=jax.ShapeDtypeStruct((s, heads * HEAD_DIM), BF16),
        grid=(heads, s // t),
        in_specs=[pl.BlockSpec((t, HEAD_DIM), lambda h, i: (i, h)),
                  pl.BlockSpec((s, HEAD_DIM), lambda h, i: (0, heads + h)),
                  pl.BlockSpec((s, HEAD_DIM), lambda h, i: (0, h)),
                  pl.BlockSpec((nkb, t, tm), lambda h, i: (0, i, 0)),
                  pl.BlockSpec((1, 2, MAX_DISTANCE, MAX_DISTANCE), lambda h, i: (h, 0, 0, 0))],
        out_specs=pl.BlockSpec((t, HEAD_DIM), lambda h, i: (i, h)),
        scratch_shapes=[pltpu.VMEM((t, t), F32), pltpu.VMEM((t, t), F32)],
        compiler_params=_cparams(2),
        name="dsa_attn",
    )(qk, qk, rest, mask, near)


def _sb_kernel(q_ref, k_ref, v_ref, tri_ref, o_ref, *, t, tk):
    qb = pl.program_id(1)
    q = q_ref[...]
    tri = tri_ref[...]
    scale2 = HEAD_DIM ** -0.5 * math.log2(math.e)
    per = t // tk
    small = max(per // 2, 1)

    def chunk(kb0, n, masked, carry):
        before, acc = carry
        width = n * tk
        rows = pl.ds(pl.multiple_of(kb0 * tk, tk), width)
        z = lax.dot_general(q, k_ref[rows, :], (((1,), (1,)), ((), ())), preferred_element_type=F32) * scale2
        lo_z, hi_z = jnp.minimum(z, 0.0), jnp.maximum(z, 0.0)
        lg = jnp.log2(1.0 + jnp.exp2(lo_z - hi_z))
        sp = hi_z + lg
        log_beta = lo_z - lg
        if masked:
            strict = kb0 * tk + lax.broadcasted_iota(jnp.int32, (t, width), 1) < row
            sp = jnp.where(strict, sp, 0.0)
        hi = sp.astype(BF16)
        lo = (sp - hi.astype(F32)).astype(BF16)
        cs = []
        for u in range(n):
            sl = slice(u * tk, (u + 1) * tk)
            cs.append(jnp.dot(jnp.concatenate([hi[:, sl], lo[:, sl]], axis=1), tri, preferred_element_type=F32))
        a = [None] * n
        for u in reversed(range(n)):
            a[u] = jnp.exp2(log_beta[:, u * tk:(u + 1) * tk] - (before + cs[u][:, :tk]))
            before = before + cs[u][:, tk:]
        a = jnp.concatenate(a, axis=1)
        if masked:
            a = jnp.where(strict, a, 0.0)
        acc = acc + jnp.dot(a.astype(BF16), v_ref[rows, :], preferred_element_type=F32)
        return before, acc

    def walk(n_iters, kb0_of, n, state):
        def cond(c):
            return jnp.logical_and(c[0] < n_iters, c[1])

        def body(c):
            before, acc = chunk(kb0_of(c[0]), n, False, c[2:])
            return c[0] + 1, jnp.min(before) < SB_EXIT_BITS, before, acc

        return lax.while_loop(cond, body, (jnp.int32(0),) + state[1:])

    row = qb * t + lax.broadcasted_iota(jnp.int32, (t, t), 0)
    carry = (jnp.zeros((t, tk), F32), jnp.zeros((t, HEAD_DIM), F32))
    before, acc = chunk(qb * per, per, True, carry)
    state = (jnp.int32(0), jnp.min(before) < SB_EXIT_BITS, before, acc)
    n_rest = qb * per
    n_small = jnp.minimum(n_rest, per) // small
    state = walk(n_small, lambda i: n_rest - (i + 1) * small, small, state)
    state = walk(qb - 1, lambda i: n_rest - per - (i + 1) * per, per, state)
    o_ref[...] = state[3].astype(o_ref.dtype)


def _sb_tri(tk):
    j = jnp.arange(tk)
    half = jnp.concatenate([j[:, None] > j[None, :], jnp.ones((tk, tk), bool)], axis=1)
    return jnp.concatenate([half, half], axis=0).astype(BF16)


def _sb_attn(rest, tri, *, t, tk, heads, q_blk, k_blk, v_blk):
    s = rest.shape[0]
    return pl.pallas_call(
        functools.partial(_sb_kernel, t=t, tk=tk),
        out_shape=jax.ShapeDtypeStruct((s, heads * HEAD_DIM), BF16),
        grid=(heads, s // t),
        in_specs=[pl.BlockSpec((t, HEAD_DIM), lambda h, i: (i, q_blk + h)),
                  pl.BlockSpec((s, HEAD_DIM), lambda h, i: (0, k_blk + h)),
                  pl.BlockSpec((s, HEAD_DIM), lambda h, i: (0, v_blk + h)),
                  pl.BlockSpec((2 * tk, 2 * tk), lambda h, i: (0, 0))],
        out_specs=pl.BlockSpec((t, HEAD_DIM), lambda h, i: (i, h)),
        compiler_params=_cparams(2),
        name="sb_attn",
    )(rest, rest, rest, tri)


def _rel_bucket(n):
    max_exact = N_BUCKETS // 2
    nf = jnp.maximum(n, 1).astype(F32)
    large = max_exact + (jnp.log(nf / max_exact) / math.log(MAX_DISTANCE / max_exact)
                         * (N_BUCKETS - max_exact)).astype(jnp.int32)
    large = jnp.minimum(large, N_BUCKETS - 1)
    return jnp.where(n < max_exact, n, large)


def _bias_tables(rel_bias):
    b = MAX_DISTANCE
    by_dist = (rel_bias[_rel_bucket(jnp.arange(2 * b))] - rel_bias[N_BUCKETS - 1]) * math.log2(math.e)
    r = jnp.arange(b)[:, None]
    c = jnp.arange(b)[None, :]
    near = jnp.stack([by_dist[jnp.maximum(r - c, 0)], by_dist[b + r - c]])
    return jnp.transpose(near, (3, 0, 1, 2))


def kernel(x, positions, rel_bias, ffn1_norm, ffn1_gate, ffn1_up, ffn1_down, mix_norm, w_in, pool_w,
           pool_scale, q_norm, k_norm, w_out, ffn2_norm, ffn2_gate, ffn2_up, ffn2_down):
    del positions
    b, s, d = x.shape
    assert b == 1
    depth = w_in.shape[0]
    pool_width = pool_w.shape[1] * pool_w.shape[2]
    dsa_width = (w_in.shape[2] - pool_width - IDX_HEADS * IDX_DIM - IDX_DIM - IDX_HEADS) // 6
    sb_width = dsa_width
    dsa_heads = dsa_width // HEAD_DIM
    sb_heads = sb_width // HEAD_DIM
    topk = min(INDEX_TOPK, s // 4)
    t = _pick(s, 256)
    t_att = _pick(s, 512)
    tk_sb = 128
    t_sb = t_att

    o = [int(c) for c in np.cumsum([0, pool_width, dsa_width, dsa_width, dsa_width, IDX_HEADS * IDX_DIM, IDX_DIM,
                                    IDX_HEADS, sb_width, sb_width, sb_width])]
    w_in_b = w_in.astype(BF16)
    w_sb = w_in_b[:, :, o[7]:o[10]]
    assert o[5] % LANES == 0
    w_out_b = w_out.astype(BF16)
    ffn1 = (ffn1_norm, ffn1_gate.astype(BF16), ffn1_up.astype(BF16), ffn1_down.astype(BF16))
    ffn2 = (ffn2_norm, ffn2_gate.astype(BF16), ffn2_up.astype(BF16), ffn2_down.astype(BF16))

    near = _bias_tables(rel_bias)
    tri = _sb_tri(tk_sb)

    def ffn(xin, weights, layer):
        norm, wg, wu, wd = weights
        xn = _rmsnorm(xin, norm[layer])
        act = _matmul(xn, [wg, wu], layer, mode="swiglu", out_dtype=BF16, tm=1024, tn=512, name="ffn_up")
        return _matmul(act, [wd], layer, mode="residual", out_dtype=F32, tm=512, tn=512, extra=xin,
                       res_scale=0.5, name="ffn_down")

    h = x[0]
    for i in range(depth):
        h = ffn(h, ffn1, i)

        hn = _rmsnorm(h, mix_norm[i])
        gain = jnp.concatenate([jnp.tile(q_norm[i] * DSA_LOGIT_SCALE2, dsa_heads),
                                jnp.tile(k_norm[i], dsa_heads)])[None, :]
        u_pool = _matmul(hn, [w_in_b], i, mode="cast", out_dtype=F32, tm=1024, tn=512, col0=o[0], n=pool_width,
                         name="proj_pool")
        qk = _matmul(hn, [w_in_b], i, mode="headnorm", out_dtype=BF16, tm=1024, tn=512, col0=o[1],
                     n=2 * dsa_width, extra=gain, name="proj_qk")
        v_qi = _matmul(hn, [w_in_b], i, mode="cast", out_dtype=BF16, tm=1024, tn=512, col0=o[3],
                       n=o[5] - o[3], name="proj_v_qi")
        small = _matmul(hn, [w_in_b], i, mode="cast", out_dtype=F32, tm=1024, tn=LANES, col0=o[5], n=LANES,
                        name="proj_small")
        qkv_sb = _matmul(hn, [w_sb], i, mode="cast", out_dtype=BF16, tm=1024, tn=512, name="proj_sb")

        y_pool = _pool_mixer(u_pool, pool_w[i], pool_scale[i])

        qi_hm = jnp.transpose(v_qi[:, dsa_width:].reshape(s, IDX_HEADS, IDX_DIM), (1, 0, 2))
        kit = jnp.transpose(small[:, :IDX_DIM].astype(BF16).reshape(s // t, t, IDX_DIM), (0, 2, 1))
        wi = small[:, IDX_DIM:IDX_DIM + IDX_HEADS]
        mask = _dsa_index(qi_hm, kit, wi, t=t, topk=topk)
        y_dsa = _dsa_attn(qk, v_qi, mask, near, t=t_att, heads=dsa_heads)

        y_sb = _sb_attn(qkv_sb, tri, t=t_sb, tk=tk_sb, heads=sb_heads, q_blk=0, k_blk=sb_heads,
                        v_blk=2 * sb_heads)

        y = jnp.concatenate([y_pool, y_dsa, y_sb], axis=1)
        h = _matmul(y, [w_out_b], i, mode="residual", out_dtype=F32, tm=1024, tn=512, extra=h, name="mix_out")

        h = ffn(h, ffn2, i)
    return h[None]
```

```python
import functools
import math

import jax
import jax.numpy as jnp
import numpy as np
from jax import lax
from jax.experimental import pallas as pl
from jax.experimental.pallas import tpu as pltpu

HEAD_DIM = 128
POOL_WINDOWS = (2, 4, 8, 16)
IDX_HEADS = 16
IDX_DIM = 64
INDEX_TOPK = 256
N_BUCKETS = 32
MAX_DISTANCE = 128
EPS = 1e-6

LANES = 128
SUBLANES = 8
POOL_HISTORY = 16
NEG = -1e30
INT_MIN = -(2 ** 31)
DSA_LOGIT_SCALE2 = HEAD_DIM ** -0.5 * math.log2(math.e)
SB_EXIT_BITS = 160.0
VMEM_LIMIT = 56 * 1024 * 1024

F32 = jnp.float32
BF16 = jnp.bfloat16


def _cparams(n_grid):
    return pltpu.CompilerParams(dimension_semantics=("arbitrary",) * n_grid,
                                vmem_limit_bytes=VMEM_LIMIT)


def _pick(n, pref):
    t = min(pref, n)
    while n % t:
        t //= 2
    return t


def _rmsnorm_kernel(x_ref, g_ref, o_ref):
    x = x_ref[...]
    ms = jnp.mean(x * x, axis=-1, keepdims=True)
    o_ref[...] = (x * lax.rsqrt(ms + EPS) * g_ref[...]).astype(o_ref.dtype)


def _rmsnorm(x, g):
    s, d = x.shape
    tm = _pick(s, 256)
    return pl.pallas_call(
        _rmsnorm_kernel,
        out_shape=jax.ShapeDtypeStruct((s, d), BF16),
        grid=(s // tm,),
        in_specs=[pl.BlockSpec((tm, d), lambda i: (i, 0)),
                  pl.BlockSpec((1, d), lambda i: (0, 0))],
        out_specs=pl.BlockSpec((tm, d), lambda i: (i, 0)),
        compiler_params=_cparams(1),
        name="rmsnorm",
    )(x, g.reshape(1, d))


def _mm_kernel(a_ref, *refs, n_w, mode, res_scale):
    w_refs, extra, o_ref, wb_refs = refs[:n_w], refs[n_w:-1 - n_w], refs[-1 - n_w], refs[-n_w:]

    @pl.when(pl.program_id(1) == 0)
    def _():
        for w_ref, wb_ref in zip(w_refs, wb_refs):
            wb_ref[...] = w_ref[...].astype(BF16)

    a = a_ref[...]
    if mode == "swiglu":
        g = jnp.dot(a, wb_refs[0][...], preferred_element_type=F32)
        u = jnp.dot(a, wb_refs[1][...], preferred_element_type=F32)
        o_ref[...] = (g / (1.0 + jnp.exp(-g)) * u).astype(o_ref.dtype)
        return
    acc = jnp.dot(a, wb_refs[0][...], preferred_element_type=F32)
    if mode == "cast":
        o_ref[...] = acc.astype(o_ref.dtype)
    elif mode == "residual":
        o_ref[...] = extra[0][...] + res_scale * acc
    elif mode == "headnorm":
        gain = extra[0][...]
        for c in range(acc.shape[1] // HEAD_DIM):
            sl = slice(c * HEAD_DIM, (c + 1) * HEAD_DIM)
            y = acc[:, sl]
            ms = jnp.mean(y * y, axis=-1, keepdims=True)
            o_ref[:, sl] = (y * lax.rsqrt(ms + EPS) * gain[:, sl]).astype(o_ref.dtype)
    else:
        raise ValueError(mode)


def _matmul(a, ws, layer, *, mode, out_dtype, tm, tn, col0=0, n=None, extra=None, res_scale=1.0, name):
    m, k = a.shape
    n = ws[0].shape[2] if n is None else n
    tm, tn = _pick(m, tm), _pick(n, tn)
    assert col0 % tn == 0
    in_specs = [pl.BlockSpec((tm, k), lambda j, i: (i, 0))]
    in_specs += [pl.BlockSpec((None, k, tn), lambda j, i: (layer, 0, col0 // tn + j)) for _ in ws]
    operands = [a] + list(ws)
    if mode == "residual":
        in_specs.append(pl.BlockSpec((tm, tn), lambda j, i: (i, j)))
        operands.append(extra)
    elif mode == "headnorm":
        in_specs.append(pl.BlockSpec((1, tn), lambda j, i: (0, j)))
        operands.append(extra)
    return pl.pallas_call(
        functools.partial(_mm_kernel, n_w=len(ws), mode=mode, res_scale=res_scale),
        out_shape=jax.ShapeDtypeStruct((m, n), out_dtype),
        grid=(n // tn, m // tm),
        in_specs=in_specs,
        out_specs=pl.BlockSpec((tm, tn), lambda j, i: (i, j)),
        scratch_shapes=[pltpu.VMEM((k, tn), BF16) for _ in ws],
        compiler_params=_cparams(2),
        name=name,
    )(*operands)


def _pool_kernel(cur_ref, prev_ref, w_ref, scale_ref, o_ref, *, tm, gd):
    i = pl.program_id(0)
    cur = cur_ref[...]
    prev = jnp.where(i > 0, prev_ref[...], 0.0)
    ext = jnp.concatenate([prev, cur], axis=0)
    t = i * tm + lax.broadcasted_iota(jnp.int32, (tm, 1), 0)
    sums = {1: ext}
    w = 1
    while w < max(POOL_WINDOWS):
        sums[2 * w] = sums[w] + pltpu.roll(sums[w], w, axis=0)
        w *= 2
    for g, win in enumerate(POOL_WINDOWS):
        sl = slice(g * gd, (g + 1) * gd)
        cnt = jnp.minimum(t + 1, win).astype(F32)
        d = sums[win][POOL_HISTORY:, sl] / cnt - cur[:, sl]
        y = jnp.dot(d.astype(BF16), w_ref[g], preferred_element_type=F32)
        o_ref[:, sl] = (y * scale_ref[:, sl]).astype(o_ref.dtype)


def _pool_mixer(u_pool, pool_w, pool_scale):
    s, pw = u_pool.shape
    groups, gd, _ = pool_w.shape
    tm = _pick(s, 512)
    hist_blocks = tm // POOL_HISTORY
    return pl.pallas_call(
        functools.partial(_pool_kernel, tm=tm, gd=gd),
        out_shape=jax.ShapeDtypeStruct((s, pw), BF16),
        grid=(s // tm,),
        in_specs=[pl.BlockSpec((tm, pw), lambda i: (i, 0)),
                  pl.BlockSpec((POOL_HISTORY, pw), lambda i: (jnp.maximum(i * hist_blocks - 1, 0), 0)),
                  pl.BlockSpec((groups, gd, gd), lambda i: (0, 0, 0)),
                  pl.BlockSpec((1, pw), lambda i: (0, 0))],
        out_specs=pl.BlockSpec((tm, pw), lambda i: (i, 0)),
        compiler_params=_cparams(1),
        name="pool_mixer",
    )(u_pool, u_pool, pool_w.astype(BF16), pool_scale.reshape(1, pw))


def _dsa_index_kernel(qi_ref, kit_ref, w_ref, o_ref, keys_ref, keyst_ref, hit_ref, wb_ref, *, t, topk):
    qb = pl.program_id(0)
    nkb = o_ref.shape[0]
    w = w_ref[...] * (IDX_DIM ** -0.5 * IDX_HEADS ** -0.5)
    for h in range(IDX_HEADS):
        wb_ref[h] = jnp.broadcast_to(w[:, h:h + 1], (t, t))
    row = qb * t + lax.broadcasted_iota(jnp.int32, (t, t), 0)
    col0 = lax.broadcasted_iota(jnp.int32, (t, t), 1)

    def score_tile(kb, carry):
        kt = kit_ref[kb]
        acc = jnp.zeros((t, t), F32)
        for h in range(IDX_HEADS):
            sc = jnp.dot(qi_ref[h], kt, preferred_element_type=F32)
            acc = acc + jnp.maximum(sc, 0.0) * wb_ref[h]
        acc = jnp.where(acc == 0.0, 0.0, acc)
        bits = pltpu.bitcast(acc, jnp.int32)
        key = jnp.where(bits < 0, bits ^ 0x7FFFFFFF, bits)
        key = jnp.where(kb * t + col0 <= row, key, INT_MIN)
        keys_ref[kb] = key
        key_t = key.T
        keyst_ref[kb] = key_t
        bits_t = jnp.where(key_t < 0, key_t ^ 0x7FFFFFFF, key_t) & -65536
        hit_ref[kb] = pltpu.bitcast(bits_t, F32).astype(BF16)
        return carry

    lax.fori_loop(0, qb + 1, score_tile, 0)

    n_acc = 4
    pack = 2 * SUBLANES

    def hi_step(it, thr16):
        cand16 = thr16 + jnp.left_shift(jnp.int32(1), 15 - it)
        cand_bits = jnp.left_shift(jnp.where(cand16 < 0, cand16 ^ 0x7FFF, cand16), 16)
        cand_b = jnp.broadcast_to(pltpu.bitcast(cand_bits, F32).astype(BF16), (pack, t))

        def count_tile(kb, cs):
            cs = list(cs)
            for g in range(t // pack):
                xh = hit_ref[kb, g * pack:(g + 1) * pack, :]
                cs[g % n_acc] = cs[g % n_acc] + jnp.where(xh >= cand_b, jnp.ones((), BF16), jnp.zeros((), BF16))
            return tuple(cs)

        cs = lax.fori_loop(0, qb + 1, count_tile, tuple(jnp.zeros((pack, t), BF16) for _ in range(n_acc)))
        c = (cs[0].astype(F32) + cs[1].astype(F32)) + (cs[2].astype(F32) + cs[3].astype(F32))
        cnt = jnp.sum(c, axis=0, keepdims=True)
        return jnp.where(cnt >= float(topk), cand16, thr16)

    def lo_step(it, thr):
        cand = thr + jnp.left_shift(jnp.int32(1), 15 - it)
        cand_b = jnp.broadcast_to(cand, (SUBLANES, t))

        def count_tile(kb, cs):
            cs = list(cs)
            for g in range(t // SUBLANES):
                kk = keyst_ref[kb, g * SUBLANES:(g + 1) * SUBLANES, :]
                cs[g % n_acc] = cs[g % n_acc] + jnp.where(kk >= cand_b, 1, 0)
            return tuple(cs)

        cs = lax.fori_loop(0, qb + 1, count_tile,
                           tuple(jnp.zeros((SUBLANES, t), jnp.int32) for _ in range(n_acc)))
        c = (cs[0] + cs[1]) + (cs[2] + cs[3])
        cnt = jnp.sum(c.astype(F32), axis=0, keepdims=True)
        return jnp.where(cnt >= float(topk), cand, thr)

    assert nkb * t <= 256 * pack * n_acc
    thr16 = lax.fori_loop(0, 16, hi_step, jnp.full((1, t), -(2 ** 15), jnp.int32))
    thr = lax.fori_loop(0, 16, lo_step, jnp.left_shift(thr16, 16))
    thr = jnp.maximum(thr, INT_MIN + 1)
    thr_b = jnp.broadcast_to(thr, (LANES, t)).T
    thr_b = jnp.concatenate([thr_b] * (t // LANES), axis=1)

    def write_tile(kb, carry):
        o_ref[kb] = jnp.where(keys_ref[kb] >= thr_b, 0.0, NEG).astype(o_ref.dtype)
        return carry

    def fill_tile(kb, carry):
        o_ref[kb] = jnp.full((t, t), NEG, o_ref.dtype)
        return carry

    lax.fori_loop(0, qb + 1, write_tile, 0)
    lax.fori_loop(qb + 1, nkb, fill_tile, 0)


def _dsa_index(qi_hm, kit, wi, *, t, topk):
    heads, s, idim = qi_hm.shape
    nkb = s // t
    return pl.pallas_call(
        functools.partial(_dsa_index_kernel, t=t, topk=topk),
        out_shape=jax.ShapeDtypeStruct((nkb, s, t), BF16),
        grid=(nkb,),
        in_specs=[pl.BlockSpec((heads, t, idim), lambda i: (0, i, 0)),
                  pl.BlockSpec((nkb, idim, t), lambda i: (0, 0, 0)),
                  pl.BlockSpec((t, heads), lambda i: (i, 0))],
        out_specs=pl.BlockSpec((nkb, t, t), lambda i: (0, i, 0)),
        scratch_shapes=[pltpu.VMEM((nkb, t, t), jnp.int32),
                        pltpu.VMEM((nkb, t, t), jnp.int32),
                        pltpu.VMEM((nkb, t, t), BF16),
                        pltpu.VMEM((heads, t, t), F32)],
        compiler_params=_cparams(1),
        name="dsa_index",
    )(qi_hm, kit, wi)


def _dsa_attn_kernel(q_ref, k_ref, v_ref, mask_ref, near_ref, o_ref, s0_ref, s1_ref, *, t, tm):
    qb = pl.program_id(1)
    q = q_ref[...]
    per = t // tm
    nb = t // MAX_DISTANCE

    def add_near(s, which):
        out_rows = []
        for i in range(nb):
            blocks = []
            for j in range(nb):
                blk = s[i * MAX_DISTANCE:(i + 1) * MAX_DISTANCE, j * MAX_DISTANCE:(j + 1) * MAX_DISTANCE]
                if which == "diag" and i == j:
                    blk = blk + near_ref[0, 0]
                elif (which == "diag" and i == j + 1) or (which == "before" and i == 0 and j == nb - 1):
                    blk = blk + near_ref[0, 1]
                blocks.append(blk)
            out_rows.append(jnp.concatenate(blocks, axis=1))
        return jnp.concatenate(out_rows, axis=0)

    bufs = (s0_ref, s1_ref)

    def logits_into(kc, dst_ref):
        rows = pl.ds(pl.multiple_of(kc * t, t), t)
        dst_ref[...] = lax.dot_general(q, k_ref[rows, :], (((1,), (1,)), ((), ())), preferred_element_type=F32)

    def step(kc, which, parity, carry):
        if which != "diag":
            logits_into(kc + 1, bufs[1 - parity])
        m, l, acc = carry
        s = bufs[parity][...] + jnp.concatenate([mask_ref[kc * per + j] for j in range(per)],
                                                axis=1).astype(F32)
        if which != "far":
            s = add_near(s, which)
        m_new = jnp.maximum(m, jnp.max(s, axis=1, keepdims=True))
        alpha = jnp.exp2(m - m_new)
        p = jnp.exp2(s - m_new)
        l = alpha * l + jnp.sum(p, axis=1, keepdims=True)
        rows = pl.ds(pl.multiple_of(kc * t, t), t)
        acc = alpha * acc + jnp.dot(p.astype(BF16), v_ref[rows, :], preferred_element_type=F32)
        return m_new, l, acc

    def far_pair(i, carry):
        carry = step(2 * i, "far", 0, carry)
        return step(2 * i + 1, "far", 1, carry)

    def tail(kinds):
        def run(k0, carry):
            for n, which in enumerate(kinds):
                carry = step(k0 + n, which, n % 2, carry)
            return carry
        return run

    logits_into(0, s0_ref)
    n_far = jnp.maximum(qb - 1, 0)
    carry = (jnp.full((t, 1), NEG, F32), jnp.zeros((t, 1), F32), jnp.zeros((t, HEAD_DIM), F32))
    carry = lax.fori_loop(0, n_far // 2, far_pair, carry)
    k0 = 2 * (n_far // 2)
    case = jnp.where(qb == 0, 0, 1 + n_far % 2)
    _, l, acc = lax.switch(case, [tail(("diag",)), tail(("before", "diag")), tail(("far", "before", "diag"))],
                           k0, carry)
    o_ref[...] = (acc / l).astype(o_ref.dtype)


def _dsa_attn(qk, rest, mask, near, *, t, heads):
    s = qk.shape[0]
    nkb, _, tm = mask.shape
    return pl.pallas_call(
        functools.partial(_dsa_attn_kernel, t=t, tm=tm),
        out_shape=jax.ShapeDtypeStruct((s, heads * HEAD_DIM), BF16),
        grid=(heads, s // t),
        in_specs=[pl.BlockSpec((t, HEAD_DIM), lambda h, i: (i, h)),
                  pl.BlockSpec((s, HEAD_DIM), lambda h, i: (0, heads + h)),
                  pl.BlockSpec((s, HEAD_DIM), lambda h, i: (0, h)),
                  pl.BlockSpec((nkb, t, tm), lambda h, i: (0, i, 0)),
                  pl.BlockSpec((1, 2, MAX_DISTANCE, MAX_DISTANCE), lambda h, i: (h, 0, 0, 0))],
        out_specs=pl.BlockSpec((t, HEAD_DIM), lambda h, i: (i, h)),
        scratch_shapes=[pltpu.VMEM((t, t), F32), pltpu.VMEM((t, t), F32)],
        compiler_params=_cparams(2),
        name="dsa_attn",
    )(qk, qk, rest, mask, near)


def _sb_kernel(q_ref, k_ref, v_ref, tri_ref, o_ref, *, t, tk):
    qb = pl.program_id(1)
    q = q_ref[...]
    tri = tri_ref[...]
    scale2 = HEAD_DIM ** -0.5 * math.log2(math.e)
    per = t // tk
    small = max(per // 2, 1)

    def chunk(kb0, n, masked, carry):
        before, acc = carry
        width = n * tk
        rows = pl.ds(pl.multiple_of(kb0 * tk, tk), width)
        z = lax.dot_general(q, k_ref[rows, :], (((1,), (1,)), ((), ())), preferred_element_type=F32) * scale2
        lo_z, hi_z = jnp.minimum(z, 0.0), jnp.maximum(z, 0.0)
        lg = jnp.log2(1.0 + jnp.exp2(lo_z - hi_z))
        sp = hi_z + lg
        log_beta = lo_z - lg
        if masked:
            strict = kb0 * tk + lax.broadcasted_iota(jnp.int32, (t, width), 1) < row
            sp = jnp.where(strict, sp, 0.0)
        hi = sp.astype(BF16)
        lo = (sp - hi.astype(F32)).astype(BF16)
        cs = []
        for u in range(n):
            sl = slice(u * tk, (u + 1) * tk)
            cs.append(jnp.dot(jnp.concatenate([hi[:, sl], lo[:, sl]], axis=1), tri, preferred_element_type=F32))
        a = [None] * n
        for u in reversed(range(n)):
            a[u] = jnp.exp2(log_beta[:, u * tk:(u + 1) * tk] - (before + cs[u][:, :tk]))
            before = before + cs[u][:, tk:]
        a = jnp.concatenate(a, axis=1)
        if masked:
            a = jnp.where(strict, a, 0.0)
        acc = acc + jnp.dot(a.astype(BF16), v_ref[rows, :], preferred_element_type=F32)
        return before, acc

    def walk(n_iters, kb0_of, n, state):
        def cond(c):
            return jnp.logical_and(c[0] < n_iters, c[1])

        def body(c):
            before, acc = chunk(kb0_of(c[0]), n, False, c[2:])
            return c[0] + 1, jnp.min(before) < SB_EXIT_BITS, before, acc

        return lax.while_loop(cond, body, (jnp.int32(0),) + state[1:])

    row = qb * t + lax.broadcasted_iota(jnp.int32, (t, t), 0)
    carry = (jnp.zeros((t, tk), F32), jnp.zeros((t, HEAD_DIM), F32))
    before, acc = chunk(qb * per, per, True, carry)
    state = (jnp.int32(0), jnp.min(before) < SB_EXIT_BITS, before, acc)
    n_rest = qb * per
    n_small = jnp.minimum(n_rest, per) // small
    state = walk(n_small, lambda i: n_rest - (i + 1) * small, small, state)
    state = walk(qb - 1, lambda i: n_rest - per - (i + 1) * per, per, state)
    o_ref[...] = state[3].astype(o_ref.dtype)


def _sb_tri(tk):
    j = jnp.arange(tk)
    half = jnp.concatenate([j[:, None] > j[None, :], jnp.ones((tk, tk), bool)], axis=1)
    return jnp.concatenate([half, half], axis=0).astype(BF16)


def _sb_attn(rest, tri, *, t, tk, heads, q_blk, k_blk, v_blk):
    s = rest.shape[0]
    return pl.pallas_call(
        functools.partial(_sb_kernel, t=t, tk=tk),
        out_shape=jax.ShapeDtypeStruct((s, heads * HEAD_DIM), BF16),
        grid=(heads, s // t),
        in_specs=[pl.BlockSpec((t, HEAD_DIM), lambda h, i: (i, q_blk + h)),
                  pl.BlockSpec((s, HEAD_DIM), lambda h, i: (0, k_blk + h)),
                  pl.BlockSpec((s, HEAD_DIM), lambda h, i: (0, v_blk + h)),
                  pl.BlockSpec((2 * tk, 2 * tk), lambda h, i: (0, 0))],
        out_specs=pl.BlockSpec((t, HEAD_DIM), lambda h, i: (i, h)),
        compiler_params=_cparams(2),
        name="sb_attn",
    )(rest, rest, rest, tri)


def _rel_bucket(n):
    max_exact = N_BUCKETS // 2
    nf = jnp.maximum(n, 1).astype(F32)
    large = max_exact + (jnp.log(nf / max_exact) / math.log(MAX_DISTANCE / max_exact)
                         * (N_BUCKETS - max_exact)).astype(jnp.int32)
    large = jnp.minimum(large, N_BUCKETS - 1)
    return jnp.where(n < max_exact, n, large)


def _bias_tables(rel_bias):
    b = MAX_DISTANCE
    by_dist = (rel_bias[_rel_bucket(jnp.arange(2 * b))] - rel_bias[N_BUCKETS - 1]) * math.log2(math.e)
    r = jnp.arange(b)[:, None]
    c = jnp.arange(b)[None, :]
    near = jnp.stack([by_dist[jnp.maximum(r - c, 0)], by_dist[b + r - c]])
    return jnp.transpose(near, (3, 0, 1, 2))


def kernel(x, positions, rel_bias, ffn1_norm, ffn1_gate, ffn1_up, ffn1_down, mix_norm, w_in, pool_w,
           pool_scale, q_norm, k_norm, w_out, ffn2_norm, ffn2_gate, ffn2_up, ffn2_down):
    del positions
    b, s, d = x.shape
    assert b == 1
    depth = w_in.shape[0]
    pool_width = pool_w.shape[1] * pool_w.shape[2]
    dsa_width = (w_in.shape[2] - pool_width - IDX_HEADS * IDX_DIM - IDX_DIM - IDX_HEADS) // 6
    sb_width = dsa_width
    dsa_heads = dsa_width // HEAD_DIM
    sb_heads = sb_width // HEAD_DIM
    topk = min(INDEX_TOPK, s // 4)
    t = _pick(s, 256)
    t_att = _pick(s, 512)
    tk_sb = 128
    t_sb = t_att

    o = [int(c) for c in np.cumsum([0, pool_width, dsa_width, dsa_width, dsa_width, IDX_HEADS * IDX_DIM, IDX_DIM,
                                    IDX_HEADS, sb_width, sb_width, sb_width])]
    w_sb = w_in[:, :, o[7]:o[10]]
    assert o[5] % LANES == 0
    ffn1 = (ffn1_norm, ffn1_gate, ffn1_up, ffn1_down)
    ffn2 = (ffn2_norm, ffn2_gate, ffn2_up, ffn2_down)

    near = _bias_tables(rel_bias)
    tri = _sb_tri(tk_sb)

    def ffn(xin, weights, layer):
        norm, wg, wu, wd = weights
        xn = _rmsnorm(xin, norm[layer])
        act = _matmul(xn, [wg, wu], layer, mode="swiglu", out_dtype=BF16, tm=1024, tn=256, name="ffn_up")
        return _matmul(act, [wd], layer, mode="residual", out_dtype=F32, tm=512, tn=512, extra=xin,
                       res_scale=0.5, name="ffn_down")

    h = x[0]
    for i in range(depth):
        h = ffn(h, ffn1, i)

        hn = _rmsnorm(h, mix_norm[i])
        gain = jnp.concatenate([jnp.tile(q_norm[i] * DSA_LOGIT_SCALE2, dsa_heads),
                                jnp.tile(k_norm[i], dsa_heads)])[None, :]
        u_pool = _matmul(hn, [w_in], i, mode="cast", out_dtype=F32, tm=1024, tn=512, col0=o[0], n=pool_width,
                         name="proj_pool")
        qk = _matmul(hn, [w_in], i, mode="headnorm", out_dtype=BF16, tm=1024, tn=512, col0=o[1],
                     n=2 * dsa_width, extra=gain, name="proj_qk")
        v_qi = _matmul(hn, [w_in], i, mode="cast", out_dtype=BF16, tm=1024, tn=512, col0=o[3],
                       n=o[5] - o[3], name="proj_v_qi")
        small = _matmul(hn, [w_in], i, mode="cast", out_dtype=F32, tm=1024, tn=LANES, col0=o[5], n=LANES,
                        name="proj_small")
        qkv_sb = _matmul(hn, [w_sb], i, mode="cast", out_dtype=BF16, tm=1024, tn=512, name="proj_sb")

        y_pool = _pool_mixer(u_pool, pool_w[i], pool_scale[i])

        qi_hm = jnp.transpose(v_qi[:, dsa_width:].reshape(s, IDX_HEADS, IDX_DIM), (1, 0, 2))
        kit = jnp.transpose(small[:, :IDX_DIM].astype(BF16).reshape(s // t, t, IDX_DIM), (0, 2, 1))
        wi = small[:, IDX_DIM:IDX_DIM + IDX_HEADS]
        mask = _dsa_index(qi_hm, kit, wi, t=t, topk=topk)
        y_dsa = _dsa_attn(qk, v_qi, mask, near, t=t_att, heads=dsa_heads)

        y_sb = _sb_attn(qkv_sb, tri, t=t_sb, tk=tk_sb, heads=sb_heads, q_blk=0, k_blk=sb_heads,
                        v_blk=2 * sb_heads)

        y = jnp.concatenate([y_pool, y_dsa, y_sb], axis=1)
        h = _matmul(y, [w_out], i, mode="residual", out_dtype=F32, tm=1024, tn=512, extra=h, name="mix_out")

        h = ffn(h, ffn2, i)
    return h[None]
```

```python
import functools
import math

import jax
import jax.numpy as jnp
import numpy as np
from jax import lax
from jax.experimental import pallas as pl
from jax.experimental.pallas import tpu as pltpu

HEAD_DIM = 128
POOL_WINDOWS = (2, 4, 8, 16)
IDX_HEADS = 16
IDX_DIM = 64
INDEX_TOPK = 256
N_BUCKETS = 32
MAX_DISTANCE = 128
EPS = 1e-6

LANES = 128
SUBLANES = 8
POOL_HISTORY = 16
NEG = -1e30
INT_MIN = -(2 ** 31)
F32_MIN_NORMAL = 2.0 ** -126
BF16_MIN_NORMAL_KEY = 0x0080
DSA_LOGIT_SCALE2 = HEAD_DIM ** -0.5 * math.log2(math.e)
SB_EXIT_BITS = 160.0
VMEM_LIMIT = 56 * 1024 * 1024

F32 = jnp.float32
BF16 = jnp.bfloat16


def _cparams(n_grid):
    return pltpu.CompilerParams(dimension_semantics=("arbitrary",) * n_grid,
                                vmem_limit_bytes=VMEM_LIMIT)


def _pick(n, pref):
    t = min(pref, n)
    while n % t:
        t //= 2
    return t


def _rmsnorm_kernel(x_ref, g_ref, o_ref):
    x = x_ref[...]
    ms = jnp.mean(x * x, axis=-1, keepdims=True)
    o_ref[...] = (x * lax.rsqrt(ms + EPS) * g_ref[...]).astype(o_ref.dtype)


def _rmsnorm(x, g):
    s, d = x.shape
    tm = _pick(s, 256)
    return pl.pallas_call(
        _rmsnorm_kernel,
        out_shape=jax.ShapeDtypeStruct((s, d), BF16),
        grid=(s // tm,),
        in_specs=[pl.BlockSpec((tm, d), lambda i: (i, 0)),
                  pl.BlockSpec((1, d), lambda i: (0, 0))],
        out_specs=pl.BlockSpec((tm, d), lambda i: (i, 0)),
        compiler_params=_cparams(1),
        name="rmsnorm",
    )(x, g.reshape(1, d))


def _mm_kernel(a_ref, *refs, n_w, mode, res_scale):
    w_refs, extra, o_ref, wb_refs = refs[:n_w], refs[n_w:-1 - n_w], refs[-1 - n_w], refs[-n_w:]

    @pl.when(pl.program_id(1) == 0)
    def _():
        for w_ref, wb_ref in zip(w_refs, wb_refs):
            wb_ref[...] = w_ref[...].astype(BF16)

    a = a_ref[...]
    if mode == "swiglu":
        g = jnp.dot(a, wb_refs[0][...], preferred_element_type=F32)
        u = jnp.dot(a, wb_refs[1][...], preferred_element_type=F32)
        o_ref[...] = (g / (1.0 + jnp.exp(-g)) * u).astype(o_ref.dtype)
        return
    acc = jnp.dot(a, wb_refs[0][...], preferred_element_type=F32)
    if mode == "cast":
        o_ref[...] = acc.astype(o_ref.dtype)
    elif mode == "residual":
        o_ref[...] = extra[0][...] + res_scale * acc
    elif mode == "headnorm":
        gain = extra[0][...]
        for c in range(acc.shape[1] // HEAD_DIM):
            sl = slice(c * HEAD_DIM, (c + 1) * HEAD_DIM)
            y = acc[:, sl]
            ms = jnp.mean(y * y, axis=-1, keepdims=True)
            o_ref[:, sl] = (y * lax.rsqrt(ms + EPS) * gain[:, sl]).astype(o_ref.dtype)
    else:
        raise ValueError(mode)


def _matmul(a, ws, layer, *, mode, out_dtype, tm, tn, col0=0, n=None, extra=None, res_scale=1.0, name):
    m, k = a.shape
    n = ws[0].shape[2] if n is None else n
    tm, tn = _pick(m, tm), _pick(n, tn)
    assert col0 % tn == 0
    in_specs = [pl.BlockSpec((tm, k), lambda j, i: (i, 0))]
    in_specs += [pl.BlockSpec((None, k, tn), lambda j, i: (layer, 0, col0 // tn + j)) for _ in ws]
    operands = [a] + list(ws)
    if mode == "residual":
        in_specs.append(pl.BlockSpec((tm, tn), lambda j, i: (i, j)))
        operands.append(extra)
    elif mode == "headnorm":
        in_specs.append(pl.BlockSpec((1, tn), lambda j, i: (0, j)))
        operands.append(extra)
    return pl.pallas_call(
        functools.partial(_mm_kernel, n_w=len(ws), mode=mode, res_scale=res_scale),
        out_shape=jax.ShapeDtypeStruct((m, n), out_dtype),
        grid=(n // tn, m // tm),
        in_specs=in_specs,
        out_specs=pl.BlockSpec((tm, tn), lambda j, i: (i, j)),
        scratch_shapes=[pltpu.VMEM((k, tn), BF16) for _ in ws],
        compiler_params=_cparams(2),
        name=name,
    )(*operands)


def _pool_kernel(cur_ref, prev_ref, w_ref, scale_ref, o_ref, *, tm, gd):
    i = pl.program_id(0)
    cur = cur_ref[...]
    prev = jnp.where(i > 0, prev_ref[...], 0.0)
    ext = jnp.concatenate([prev, cur], axis=0)
    t = i * tm + lax.broadcasted_iota(jnp.int32, (tm, 1), 0)
    sums = {1: ext}
    w = 1
    while w < max(POOL_WINDOWS):
        sums[2 * w] = sums[w] + pltpu.roll(sums[w], w, axis=0)
        w *= 2
    for g, win in enumerate(POOL_WINDOWS):
        sl = slice(g * gd, (g + 1) * gd)
        cnt = jnp.minimum(t + 1, win).astype(F32)
        d = sums[win][POOL_HISTORY:, sl] / cnt - cur[:, sl]
        y = jnp.dot(d.astype(BF16), w_ref[g], preferred_element_type=F32)
        o_ref[:, sl] = (y * scale_ref[:, sl]).astype(o_ref.dtype)


def _pool_mixer(u_pool, pool_w, pool_scale):
    s, pw = u_pool.shape
    groups, gd, _ = pool_w.shape
    tm = _pick(s, 512)
    hist_blocks = tm // POOL_HISTORY
    return pl.pallas_call(
        functools.partial(_pool_kernel, tm=tm, gd=gd),
        out_shape=jax.ShapeDtypeStruct((s, pw), BF16),
        grid=(s // tm,),
        in_specs=[pl.BlockSpec((tm, pw), lambda i: (i, 0)),
                  pl.BlockSpec((POOL_HISTORY, pw), lambda i: (jnp.maximum(i * hist_blocks - 1, 0), 0)),
                  pl.BlockSpec((groups, gd, gd), lambda i: (0, 0, 0)),
                  pl.BlockSpec((1, pw), lambda i: (0, 0))],
        out_specs=pl.BlockSpec((tm, pw), lambda i: (i, 0)),
        compiler_params=_cparams(1),
        name="pool_mixer",
    )(u_pool, u_pool, pool_w.astype(BF16), pool_scale.reshape(1, pw))


def _dsa_index_kernel(qi_ref, kit_ref, w_ref, o_ref, keys_ref, keyst_ref, hit_ref, wb_ref, *, t, topk):
    qb = pl.program_id(0)
    nkb = o_ref.shape[0]
    w = w_ref[...] * (IDX_DIM ** -0.5 * IDX_HEADS ** -0.5)
    for h in range(IDX_HEADS):
        wb_ref[h] = jnp.broadcast_to(w[:, h:h + 1], (t, t))
    row = qb * t + lax.broadcasted_iota(jnp.int32, (t, t), 0)
    col0 = lax.broadcasted_iota(jnp.int32, (t, t), 1)

    def score_tile(kb, carry):
        kt = kit_ref[kb]
        acc = jnp.zeros((t, t), F32)
        for h in range(IDX_HEADS):
            sc = jnp.dot(qi_ref[h], kt, preferred_element_type=F32)
            acc = acc + jnp.maximum(sc, 0.0) * wb_ref[h]
        acc = jnp.where(jnp.abs(acc) < F32_MIN_NORMAL, 0.0, acc)
        bits = pltpu.bitcast(acc, jnp.int32)
        key = jnp.where(bits < 0, bits ^ 0x7FFFFFFF, bits)
        key = jnp.where(kb * t + col0 <= row, key, INT_MIN)
        keys_ref[kb] = key
        key_t = key.T
        keyst_ref[kb] = key_t
        bits_t = jnp.where(key_t < 0, key_t ^ 0x7FFFFFFF, key_t) & -65536
        hit_ref[kb] = pltpu.bitcast(bits_t, F32).astype(BF16)
        return carry

    lax.fori_loop(0, qb + 1, score_tile, 0)

    n_acc = 4
    pack = 2 * SUBLANES

    def hi_step(it, state):
        thr16, kept = state
        cand16 = thr16 + jnp.left_shift(jnp.int32(1), 15 - it)
        c16 = jnp.where(jnp.logical_and(cand16 > 0, cand16 < BF16_MIN_NORMAL_KEY), BF16_MIN_NORMAL_KEY, cand16)
        c16 = jnp.where(jnp.logical_and(c16 < 0, c16 >= -BF16_MIN_NORMAL_KEY), 0, c16)
        cand_bits = jnp.left_shift(jnp.where(c16 < 0, c16 ^ 0x7FFF, c16), 16)
        cand_b = jnp.broadcast_to(pltpu.bitcast(cand_bits, F32).astype(BF16), (pack, t))

        def count_tile(kb, cs):
            cs = list(cs)
            for g in range(t // pack):
                xh = hit_ref[kb, g * pack:(g + 1) * pack, :]
                cs[g % n_acc] = cs[g % n_acc] + jnp.where(xh >= cand_b, jnp.ones((), BF16), jnp.zeros((), BF16))
            return tuple(cs)

        cs = lax.fori_loop(0, qb + 1, count_tile, tuple(jnp.zeros((pack, t), BF16) for _ in range(n_acc)))
        c = (cs[0].astype(F32) + cs[1].astype(F32)) + (cs[2].astype(F32) + cs[3].astype(F32))
        cnt = jnp.sum(c, axis=0, keepdims=True)
        ok = cnt >= float(topk)
        return jnp.where(ok, cand16, thr16), jnp.where(ok, cnt, kept)

    def lo_step(it, state):
        thr, kept = state
        cand = thr + jnp.left_shift(jnp.int32(1), 15 - it)
        cand_b = jnp.broadcast_to(cand, (SUBLANES, t))

        def count_tile(kb, cs):
            cs = list(cs)
            for g in range(t // SUBLANES):
                kk = keyst_ref[kb, g * SUBLANES:(g + 1) * SUBLANES, :]
                cs[g % n_acc] = cs[g % n_acc] + jnp.where(kk >= cand_b, 1, 0)
            return tuple(cs)

        cs = lax.fori_loop(0, qb + 1, count_tile,
                           tuple(jnp.zeros((SUBLANES, t), jnp.int32) for _ in range(n_acc)))
        c = (cs[0] + cs[1]) + (cs[2] + cs[3])
        cnt = jnp.sum(c.astype(F32), axis=0, keepdims=True)
        ok = cnt >= float(topk)
        return jnp.where(ok, cand, thr), jnp.where(ok, cnt, kept)

    assert nkb * t <= 256 * pack * n_acc
    thr16, kept = lax.fori_loop(0, 16, hi_step,
                                (jnp.full((1, t), -(2 ** 15), jnp.int32), jnp.zeros((1, t), F32)))
    thr, kept = lax.fori_loop(0, 16, lo_step, (jnp.left_shift(thr16, 16), kept))
    thr = jnp.maximum(thr, INT_MIN + 1)

    def count_keys(pred):
        def count_tile(kb, cs):
            cs = list(cs)
            for g in range(t // SUBLANES):
                kk = keyst_ref[kb, g * SUBLANES:(g + 1) * SUBLANES, :]
                idx = kb * t + g * SUBLANES + lax.broadcasted_iota(jnp.int32, (SUBLANES, t), 0)
                cs[g % n_acc] = cs[g % n_acc] + jnp.where(pred(kk, idx), 1, 0)
            return tuple(cs)

        cs = lax.fori_loop(0, qb + 1, count_tile,
                           tuple(jnp.zeros((SUBLANES, t), jnp.int32) for _ in range(n_acc)))
        return jnp.sum(((cs[0] + cs[1]) + (cs[2] + cs[3])).astype(F32), axis=0, keepdims=True)

    def break_ties():
        thr_s = jnp.broadcast_to(thr, (SUBLANES, t))
        need = float(topk) - count_keys(lambda kk, idx: kk > thr_s)

        def idx_step(it, last):
            cand = last + jnp.left_shift(jnp.int32(1), idx_bits - 1 - it)
            cand_s = jnp.broadcast_to(cand, (SUBLANES, t))
            below = count_keys(lambda kk, idx: jnp.logical_and(kk == thr_s, idx < cand_s))
            return jnp.where(below < need, cand, last)

        return lax.fori_loop(0, idx_bits, idx_step, jnp.zeros((1, t), jnp.int32))

    idx_bits = max(1, (nkb * t - 1).bit_length())
    tied = jnp.max(jnp.where(kept > float(topk), 1.0, 0.0)) > 0.0
    last = lax.cond(tied, break_ties, lambda: jnp.full((1, t), nkb * t, jnp.int32))

    def per_row(v):
        v = jnp.broadcast_to(v, (LANES, t)).T
        return jnp.concatenate([v] * (t // LANES), axis=1)

    thr_b, last_b = per_row(thr), per_row(last)

    def write_tile(kb, carry):
        key = keys_ref[kb]
        keep = jnp.logical_or(key > thr_b, jnp.logical_and(key == thr_b, kb * t + col0 <= last_b))
        o_ref[kb] = jnp.where(keep, 0.0, NEG).astype(o_ref.dtype)
        return carry

    def fill_tile(kb, carry):
        o_ref[kb] = jnp.full((t, t), NEG, o_ref.dtype)
        return carry

    lax.fori_loop(0, qb + 1, write_tile, 0)
    lax.fori_loop(qb + 1, nkb, fill_tile, 0)


def _dsa_index(qi_hm, kit, wi, *, t, topk):
    heads, s, idim = qi_hm.shape
    nkb = s // t
    return pl.pallas_call(
        functools.partial(_dsa_index_kernel, t=t, topk=topk),
        out_shape=jax.ShapeDtypeStruct((nkb, s, t), BF16),
        grid=(nkb,),
        in_specs=[pl.BlockSpec((heads, t, idim), lambda i: (0, i, 0)),
                  pl.BlockSpec((nkb, idim, t), lambda i: (0, 0, 0)),
                  pl.BlockSpec((t, heads), lambda i: (i, 0))],
        out_specs=pl.BlockSpec((nkb, t, t), lambda i: (0, i, 0)),
        scratch_shapes=[pltpu.VMEM((nkb, t, t), jnp.int32),
                        pltpu.VMEM((nkb, t, t), jnp.int32),
                        pltpu.VMEM((nkb, t, t), BF16),
                        pltpu.VMEM((heads, t, t), F32)],
        compiler_params=_cparams(1),
        name="dsa_index",
    )(qi_hm, kit, wi)


def _dsa_attn_kernel(q_ref, k_ref, v_ref, mask_ref, near_ref, o_ref, s0_ref, s1_ref, *, t, tm):
    qb = pl.program_id(1)
    q = q_ref[...]
    per = t // tm
    nb = t // MAX_DISTANCE

    def add_near(s, which):
        out_rows = []
        for i in range(nb):
            blocks = []
            for j in range(nb):
                blk = s[i * MAX_DISTANCE:(i + 1) * MAX_DISTANCE, j * MAX_DISTANCE:(j + 1) * MAX_DISTANCE]
                if which == "diag" and i == j:
                    blk = blk + near_ref[0, 0]
                elif (which == "diag" and i == j + 1) or (which == "before" and i == 0 and j == nb - 1):
                    blk = blk + near_ref[0, 1]
                blocks.append(blk)
            out_rows.append(jnp.concatenate(blocks, axis=1))
        return jnp.concatenate(out_rows, axis=0)

    bufs = (s0_ref, s1_ref)

    def logits_into(kc, dst_ref):
        rows = pl.ds(pl.multiple_of(kc * t, t), t)
        dst_ref[...] = lax.dot_general(q, k_ref[rows, :], (((1,), (1,)), ((), ())), preferred_element_type=F32)

    def step(kc, which, parity, carry):
        if which != "diag":
            logits_into(kc + 1, bufs[1 - parity])
        m, acc = carry
        s = bufs[parity][...] + jnp.concatenate([mask_ref[kc * per + j] for j in range(per)],
                                                axis=1).astype(F32)
        if which != "far":
            s = add_near(s, which)
        m_new = jnp.maximum(m, jnp.max(s, axis=1, keepdims=True))
        p = jnp.exp2(s - m_new)
        rows = pl.ds(pl.multiple_of(kc * t, t), t)
        acc = jnp.exp2(m - m_new) * acc + jnp.dot(p.astype(BF16), v_ref[rows, :], preferred_element_type=F32)
        return m_new, acc

    def far_pair(i, carry):
        carry = step(2 * i, "far", 0, carry)
        return step(2 * i + 1, "far", 1, carry)

    def tail(kinds):
        def run(k0, carry):
            for n, which in enumerate(kinds):
                carry = step(k0 + n, which, n % 2, carry)
            return carry
        return run

    logits_into(0, s0_ref)
    n_far = jnp.maximum(qb - 1, 0)
    carry = (jnp.full((t, 1), NEG, F32), jnp.zeros((t, 2 * HEAD_DIM), F32))
    carry = lax.fori_loop(0, n_far // 2, far_pair, carry)
    k0 = 2 * (n_far // 2)
    case = jnp.where(qb == 0, 0, 1 + n_far % 2)
    _, acc = lax.switch(case, [tail(("diag",)), tail(("before", "diag")), tail(("far", "before", "diag"))],
                        k0, carry)
    o_ref[...] = (acc[:, :HEAD_DIM] / acc[:, HEAD_DIM:]).astype(o_ref.dtype)


def _dsa_attn(qk, v_ones, mask, near, *, t, heads):
    s = qk.shape[0]
    nkb, _, tm = mask.shape
    return pl.pallas_call(
        functools.partial(_dsa_attn_kernel, t=t, tm=tm),
        out_shape=jax.ShapeDtypeStruct((s, heads * HEAD_DIM), BF16),
        grid=(heads, s // t),
        in_specs=[pl.BlockSpec((t, HEAD_DIM), lambda h, i: (i, h)),
                  pl.BlockSpec((s, HEAD_DIM), lambda h, i: (0, heads + h)),
                  pl.BlockSpec((s, 2 * HEAD_DIM), lambda h, i: (0, h)),
                  pl.BlockSpec((nkb, t, tm), lambda h, i: (0, i, 0)),
                  pl.BlockSpec((1, 2, MAX_DISTANCE, MAX_DISTANCE), lambda h, i: (h, 0, 0, 0))],
        out_specs=pl.BlockSpec((t, HEAD_DIM), lambda h, i: (i, h)),
        scratch_shapes=[pltpu.VMEM((t, t), F32), pltpu.VMEM((t, t), F32)],
        compiler_params=_cparams(2),
        name="dsa_attn",
    )(qk, qk, v_ones, mask, near)


def _sb_kernel(q_ref, k_ref, v_ref, tri_ref, o_ref, *, t, tk):
    qb = pl.program_id(1)
    q = q_ref[...]
    tri = tri_ref[...]
    scale2 = HEAD_DIM ** -0.5 * math.log2(math.e)
    per = t // tk
    small = max(per // 2, 1)

    def chunk(kb0, n, masked, carry):
        before, acc = carry
        width = n * tk
        rows = pl.ds(pl.multiple_of(kb0 * tk, tk), width)
        z = lax.dot_general(q, k_ref[rows, :], (((1,), (1,)), ((), ())), preferred_element_type=F32) * scale2
        lo_z, hi_z = jnp.minimum(z, 0.0), jnp.maximum(z, 0.0)
        lg = jnp.log2(1.0 + jnp.exp2(lo_z - hi_z))
        sp = hi_z + lg
        log_beta = lo_z - lg
        if masked:
            strict = kb0 * tk + lax.broadcasted_iota(jnp.int32, (t, width), 1) < row
            sp = jnp.where(strict, sp, 0.0)
        hi = sp.astype(BF16)
        lo = (sp - hi.astype(F32)).astype(BF16)
        cs = []
        for u in range(n):
            sl = slice(u * tk, (u + 1) * tk)
            cs.append(jnp.dot(jnp.concatenate([hi[:, sl], lo[:, sl]], axis=1), tri, preferred_element_type=F32))
        a = [None] * n
        for u in reversed(range(n)):
            a[u] = jnp.exp2(log_beta[:, u * tk:(u + 1) * tk] - (before + cs[u][:, :tk]))
            before = before + cs[u][:, tk:]
        a = jnp.concatenate(a, axis=1)
        if masked:
            a = jnp.where(strict, a, 0.0)
        acc = acc + jnp.dot(a.astype(BF16), v_ref[rows, :], preferred_element_type=F32)
        return before, acc

    def walk(n_iters, kb0_of, n, state):
        def cond(c):
            return jnp.logical_and(c[0] < n_iters, c[1])

        def body(c):
            before, acc = chunk(kb0_of(c[0]), n, False, c[2:])
            return c[0] + 1, jnp.min(before) < SB_EXIT_BITS, before, acc

        return lax.while_loop(cond, body, (jnp.int32(0),) + state[1:])

    row = qb * t + lax.broadcasted_iota(jnp.int32, (t, t), 0)
    carry = (jnp.zeros((t, tk), F32), jnp.zeros((t, HEAD_DIM), F32))
    before, acc = chunk(qb * per, per, True, carry)
    state = (jnp.int32(0), jnp.min(before) < SB_EXIT_BITS, before, acc)
    n_rest = qb * per
    n_small = jnp.minimum(n_rest, per) // small
    state = walk(n_small, lambda i: n_rest - (i + 1) * small, small, state)
    state = walk(qb - 1, lambda i: n_rest - per - (i + 1) * per, per, state)
    o_ref[...] = state[3].astype(o_ref.dtype)


def _sb_tri(tk):
    j = jnp.arange(tk)
    half = jnp.concatenate([j[:, None] > j[None, :], jnp.ones((tk, tk), bool)], axis=1)
    return jnp.concatenate([half, half], axis=0).astype(BF16)


def _sb_attn(rest, tri, *, t, tk, heads, q_blk, k_blk, v_blk):
    s = rest.shape[0]
    return pl.pallas_call(
        functools.partial(_sb_kernel, t=t, tk=tk),
        out_shape=jax.ShapeDtypeStruct((s, heads * HEAD_DIM), BF16),
        grid=(heads, s // t),
        in_specs=[pl.BlockSpec((t, HEAD_DIM), lambda h, i: (i, q_blk + h)),
                  pl.BlockSpec((s, HEAD_DIM), lambda h, i: (0, k_blk + h)),
                  pl.BlockSpec((s, HEAD_DIM), lambda h, i: (0, v_blk + h)),
                  pl.BlockSpec((2 * tk, 2 * tk), lambda h, i: (0, 0))],
        out_specs=pl.BlockSpec((t, HEAD_DIM), lambda h, i: (i, h)),
        compiler_params=_cparams(2),
        name="sb_attn",
    )(rest, rest, rest, tri)


def _rel_bucket(n):
    max_exact = N_BUCKETS // 2
    nf = jnp.maximum(n, 1).astype(F32)
    large = max_exact + (jnp.log(nf / max_exact) / math.log(MAX_DISTANCE / max_exact)
                         * (N_BUCKETS - max_exact)).astype(jnp.int32)
    large = jnp.minimum(large, N_BUCKETS - 1)
    return jnp.where(n < max_exact, n, large)


def _bias_tables(rel_bias):
    b = MAX_DISTANCE
    by_dist = (rel_bias[_rel_bucket(jnp.arange(2 * b))] - rel_bias[N_BUCKETS - 1]) * math.log2(math.e)
    r = jnp.arange(b)[:, None]
    c = jnp.arange(b)[None, :]
    near = jnp.stack([by_dist[jnp.maximum(r - c, 0)], by_dist[b + r - c]])
    return jnp.transpose(near, (3, 0, 1, 2))


def kernel(x, positions, rel_bias, ffn1_norm, ffn1_gate, ffn1_up, ffn1_down, mix_norm, w_in, pool_w,
           pool_scale, q_norm, k_norm, w_out, ffn2_norm, ffn2_gate, ffn2_up, ffn2_down):
    del positions
    b, s, d = x.shape
    assert b == 1
    depth = w_in.shape[0]
    pool_width = pool_w.shape[1] * pool_w.shape[2]
    dsa_width = (w_in.shape[2] - pool_width - IDX_HEADS * IDX_DIM - IDX_DIM - IDX_HEADS) // 6
    sb_width = dsa_width
    dsa_heads = dsa_width // HEAD_DIM
    sb_heads = sb_width // HEAD_DIM
    topk = min(INDEX_TOPK, s // 4)
    t = _pick(s, 256)
    t_att = _pick(s, 512)
    tk_sb = 128
    t_sb = t_att

    o = [int(c) for c in np.cumsum([0, pool_width, dsa_width, dsa_width, dsa_width, IDX_HEADS * IDX_DIM, IDX_DIM,
                                    IDX_HEADS, sb_width, sb_width, sb_width])]
    n_idx = o[7] - o[5]
    w_tail = jnp.pad(w_in[:, :, o[10] - n_idx:], ((0, 0), (0, 0), (0, LANES - n_idx)))
    assert o[5] % LANES == 0
    ffn1 = (ffn1_norm, ffn1_gate, ffn1_up, ffn1_down)
    ffn2 = (ffn2_norm, ffn2_gate, ffn2_up, ffn2_down)

    near = _bias_tables(rel_bias)
    tri = _sb_tri(tk_sb)

    def ffn(xin, weights, layer):
        norm, wg, wu, wd = weights
        xn = _rmsnorm(xin, norm[layer])
        act = _matmul(xn, [wg, wu], layer, mode="swiglu", out_dtype=BF16, tm=1024, tn=256, name="ffn_up")
        return _matmul(act, [wd], layer, mode="residual", out_dtype=F32, tm=512, tn=512, extra=xin,
                       res_scale=0.5, name="ffn_down")

    h = x[0]
    for i in range(depth):
        h = ffn(h, ffn1, i)

        hn = _rmsnorm(h, mix_norm[i])
        gain = jnp.concatenate([jnp.tile(q_norm[i] * DSA_LOGIT_SCALE2, dsa_heads),
                                jnp.tile(k_norm[i], dsa_heads)])[None, :]
        u_pool = _matmul(hn, [w_in], i, mode="cast", out_dtype=F32, tm=1024, tn=512, col0=o[0], n=pool_width,
                         name="proj_pool")
        qk = _matmul(hn, [w_in], i, mode="headnorm", out_dtype=BF16, tm=1024, tn=512, col0=o[1],
                     n=2 * dsa_width, extra=gain, name="proj_qk")
        v_qi = _matmul(hn, [w_in], i, mode="cast", out_dtype=BF16, tm=1024, tn=512, col0=o[3],
                       n=o[5] - o[3], name="proj_v_qi")
        small = _matmul(hn, [w_in], i, mode="cast", out_dtype=F32, tm=1024, tn=LANES, col0=o[5], n=LANES,
                        name="proj_small")
        sb_main = _matmul(hn, [w_in], i, mode="cast", out_dtype=BF16, tm=1024, tn=512, col0=o[5],
                          n=3 * sb_width, name="proj_sb")
        sb_tail = _matmul(hn, [w_tail], i, mode="cast", out_dtype=BF16, tm=1024, tn=LANES, name="proj_sb_tail")
        qkv_sb = jnp.concatenate([sb_main[:, n_idx:], sb_tail[:, :n_idx]], axis=1)

        y_pool = _pool_mixer(u_pool, pool_w[i], pool_scale[i])

        qi_hm = jnp.transpose(v_qi[:, dsa_width:].reshape(s, IDX_HEADS, IDX_DIM), (1, 0, 2))
        kit = jnp.transpose(small[:, :IDX_DIM].astype(BF16).reshape(s // t, t, IDX_DIM), (0, 2, 1))
        wi = small[:, IDX_DIM:IDX_DIM + IDX_HEADS]
        mask = _dsa_index(qi_hm, kit, wi, t=t, topk=topk)
        v_dsa = v_qi[:, :dsa_width].reshape(s, dsa_heads, HEAD_DIM)
        v_ones = jnp.concatenate([v_dsa, jnp.ones_like(v_dsa)], axis=2).reshape(s, 2 * dsa_width)
        y_dsa = _dsa_attn(qk, v_ones, mask, near, t=t_att, heads=dsa_heads)

        y_sb = _sb_attn(qkv_sb, tri, t=t_sb, tk=tk_sb, heads=sb_heads, q_blk=0, k_blk=sb_heads,
                        v_blk=2 * sb_heads)

        y = jnp.concatenate([y_pool, y_dsa, y_sb], axis=1)
        h = _matmul(y, [w_out], i, mode="residual", out_dtype=F32, tm=1024, tn=512, extra=h, name="mix_out")

        h = ffn(h, ffn2, i)
    return h[None]
```

```python
import functools
import math

import jax
import jax.numpy as jnp
import numpy as np
from jax import lax
from jax.experimental import pallas as pl
from jax.experimental.pallas import tpu as pltpu

HEAD_DIM = 128
POOL_WINDOWS = (2, 4, 8, 16)
IDX_HEADS = 16
IDX_DIM = 64
INDEX_TOPK = 256
N_BUCKETS = 32
MAX_DISTANCE = 128
EPS = 1e-6

LANES = 128
SUBLANES = 8
POOL_HISTORY = 16
NEG = -1e30
INT_MIN = -(2 ** 31)
F32_MIN_NORMAL = 2.0 ** -126
BF16_MIN_NORMAL_KEY = 0x0080
DSA_LOGIT_SCALE2 = HEAD_DIM ** -0.5 * math.log2(math.e)
SB_EXIT_BITS = 160.0
VMEM_LIMIT = 56 * 1024 * 1024

F32 = jnp.float32
BF16 = jnp.bfloat16


def _cparams(n_grid):
    return pltpu.CompilerParams(dimension_semantics=("arbitrary",) * n_grid,
                                vmem_limit_bytes=VMEM_LIMIT)


def _pick(n, pref):
    t = min(pref, n)
    while n % t:
        t //= 2
    return t


def _rmsnorm_kernel(x_ref, g_ref, o_ref):
    x = x_ref[...]
    ms = jnp.mean(x * x, axis=-1, keepdims=True)
    o_ref[...] = (x * lax.rsqrt(ms + EPS) * g_ref[...]).astype(o_ref.dtype)


def _rmsnorm(x, g):
    s, d = x.shape
    tm = _pick(s, 256)
    return pl.pallas_call(
        _rmsnorm_kernel,
        out_shape=jax.ShapeDtypeStruct((s, d), BF16),
        grid=(s // tm,),
        in_specs=[pl.BlockSpec((tm, d), lambda i: (i, 0)),
                  pl.BlockSpec((1, d), lambda i: (0, 0))],
        out_specs=pl.BlockSpec((tm, d), lambda i: (i, 0)),
        compiler_params=_cparams(1),
        name="rmsnorm",
    )(x, g.reshape(1, d))


def _mm_kernel(a_ref, *refs, n_w, mode, res_scale, w_nk):
    w_refs, extra, o_ref, wb_refs = refs[:n_w], refs[n_w:-1 - n_w], refs[-1 - n_w], refs[-n_w:]

    @pl.when(pl.program_id(1) == 0)
    def _():
        for w_ref, wb_ref in zip(w_refs, wb_refs):
            wb_ref[...] = w_ref[...].astype(BF16)

    a = a_ref[...]
    contract = (((1,), (1 if w_nk else 0,)), ((), ()))

    def dot(wb_ref):
        return lax.dot_general(a, wb_ref[...], contract, preferred_element_type=F32)

    if mode == "swiglu":
        g = dot(wb_refs[0])
        u = dot(wb_refs[1])
        o_ref[...] = (g / (1.0 + jnp.exp(-g)) * u).astype(o_ref.dtype)
        return
    acc = dot(wb_refs[0])
    if mode == "cast":
        o_ref[...] = acc.astype(o_ref.dtype)
    elif mode == "residual":
        o_ref[...] = extra[0][...] + res_scale * acc
    elif mode == "headnorm":
        gain = extra[0][...]
        for c in range(acc.shape[1] // HEAD_DIM):
            sl = slice(c * HEAD_DIM, (c + 1) * HEAD_DIM)
            y = acc[:, sl]
            ms = jnp.mean(y * y, axis=-1, keepdims=True)
            o_ref[:, sl] = (y * lax.rsqrt(ms + EPS) * gain[:, sl]).astype(o_ref.dtype)
    else:
        raise ValueError(mode)


def _matmul(a, ws, layer, *, mode, out_dtype, tm, tn, col0=0, n=None, extra=None, res_scale=1.0, w_nk=False,
            name):
    m, k = a.shape
    n = ws[0].shape[1 if w_nk else 2] if n is None else n
    tm, tn = _pick(m, tm), _pick(n, tn)
    in_specs = [pl.BlockSpec((tm, k), lambda j, i: (i, 0))]
    if w_nk:
        assert col0 % SUBLANES == 0
        n_all = ws[0].shape[1]
        assert n_all % SUBLANES == 0
        ws = [w.reshape(-1, k) for w in ws]
        in_specs += [pl.BlockSpec((pl.Element(tn), pl.Element(k)),
                                  lambda j, i: (pl.multiple_of(layer * n_all + col0 + j * tn, SUBLANES), 0))
                     for _ in ws]
    else:
        assert col0 % tn == 0
        in_specs += [pl.BlockSpec((None, k, tn), lambda j, i: (layer, 0, col0 // tn + j)) for _ in ws]
    operands = [a] + list(ws)
    if mode == "residual":
        in_specs.append(pl.BlockSpec((tm, tn), lambda j, i: (i, j)))
        operands.append(extra)
    elif mode == "headnorm":
        in_specs.append(pl.BlockSpec((1, tn), lambda j, i: (0, j)))
        operands.append(extra)
    return pl.pallas_call(
        functools.partial(_mm_kernel, n_w=len(ws), mode=mode, res_scale=res_scale, w_nk=w_nk),
        out_shape=jax.ShapeDtypeStruct((m, n), out_dtype),
        grid=(n // tn, m // tm),
        in_specs=in_specs,
        out_specs=pl.BlockSpec((tm, tn), lambda j, i: (i, j)),
        scratch_shapes=[pltpu.VMEM((tn, k) if w_nk else (k, tn), BF16) for _ in ws],
        compiler_params=_cparams(2),
        name=name,
    )(*operands)


def _pool_kernel(cur_ref, prev_ref, w_ref, scale_ref, o_ref, *, tm, gd):
    i = pl.program_id(0)
    cur = cur_ref[...]
    prev = jnp.where(i > 0, prev_ref[...], 0.0)
    ext = jnp.concatenate([prev, cur], axis=0)
    t = i * tm + lax.broadcasted_iota(jnp.int32, (tm, 1), 0)
    sums = {1: ext}
    w = 1
    while w < max(POOL_WINDOWS):
        sums[2 * w] = sums[w] + pltpu.roll(sums[w], w, axis=0)
        w *= 2
    for g, win in enumerate(POOL_WINDOWS):
        sl = slice(g * gd, (g + 1) * gd)
        cnt = jnp.minimum(t + 1, win).astype(F32)
        d = sums[win][POOL_HISTORY:, sl] / cnt - cur[:, sl]
        y = jnp.dot(d.astype(BF16), w_ref[g], preferred_element_type=F32)
        o_ref[:, sl] = (y * scale_ref[:, sl]).astype(o_ref.dtype)


def _pool_mixer(u_pool, pool_w, pool_scale):
    s, pw = u_pool.shape
    groups, gd, _ = pool_w.shape
    tm = _pick(s, 512)
    hist_blocks = tm // POOL_HISTORY
    return pl.pallas_call(
        functools.partial(_pool_kernel, tm=tm, gd=gd),
        out_shape=jax.ShapeDtypeStruct((s, pw), BF16),
        grid=(s // tm,),
        in_specs=[pl.BlockSpec((tm, pw), lambda i: (i, 0)),
                  pl.BlockSpec((POOL_HISTORY, pw), lambda i: (jnp.maximum(i * hist_blocks - 1, 0), 0)),
                  pl.BlockSpec((groups, gd, gd), lambda i: (0, 0, 0)),
                  pl.BlockSpec((1, pw), lambda i: (0, 0))],
        out_specs=pl.BlockSpec((tm, pw), lambda i: (i, 0)),
        compiler_params=_cparams(1),
        name="pool_mixer",
    )(u_pool, u_pool, pool_w.astype(BF16), pool_scale.reshape(1, pw))


def _dsa_index_kernel(qi_ref, kit_ref, w_ref, o_ref, keys_ref, keyst_ref, hit_ref, wb_ref, *, t, topk):
    qb = pl.program_id(0)
    nkb = o_ref.shape[0]
    w = w_ref[...] * (IDX_DIM ** -0.5 * IDX_HEADS ** -0.5)
    for h in range(IDX_HEADS):
        wb_ref[h] = jnp.broadcast_to(w[:, h:h + 1], (t, t))
    row = qb * t + lax.broadcasted_iota(jnp.int32, (t, t), 0)
    col0 = lax.broadcasted_iota(jnp.int32, (t, t), 1)

    def score_tile(kb, carry):
        kt = kit_ref[kb]
        acc = jnp.zeros((t, t), F32)
        for h in range(IDX_HEADS):
            sc = jnp.dot(qi_ref[h], kt, preferred_element_type=F32)
            acc = acc + jnp.maximum(sc, 0.0) * wb_ref[h]
        acc = jnp.where(jnp.abs(acc) < F32_MIN_NORMAL, 0.0, acc)
        bits = pltpu.bitcast(acc, jnp.int32)
        key = jnp.where(bits < 0, bits ^ 0x7FFFFFFF, bits)
        key = jnp.where(kb * t + col0 <= row, key, INT_MIN)
        keys_ref[kb] = key
        key_t = key.T
        keyst_ref[kb] = key_t
        bits_t = jnp.where(key_t < 0, key_t ^ 0x7FFFFFFF, key_t) & -65536
        hit_ref[kb] = pltpu.bitcast(bits_t, F32).astype(BF16)
        return carry

    lax.fori_loop(0, qb + 1, score_tile, 0)

    n_acc = 4
    pack = 2 * SUBLANES

    def hi_step(it, state):
        thr16, kept = state
        cand16 = thr16 + jnp.left_shift(jnp.int32(1), 15 - it)
        c16 = jnp.where(jnp.logical_and(cand16 > 0, cand16 < BF16_MIN_NORMAL_KEY), BF16_MIN_NORMAL_KEY, cand16)
        c16 = jnp.where(jnp.logical_and(c16 < 0, c16 >= -BF16_MIN_NORMAL_KEY), 0, c16)
        cand_bits = jnp.left_shift(jnp.where(c16 < 0, c16 ^ 0x7FFF, c16), 16)
        cand_b = jnp.broadcast_to(pltpu.bitcast(cand_bits, F32).astype(BF16), (pack, t))

        def count_tile(kb, cs):
            cs = list(cs)
            for g in range(t // pack):
                xh = hit_ref[kb, g * pack:(g + 1) * pack, :]
                cs[g % n_acc] = cs[g % n_acc] + jnp.where(xh >= cand_b, jnp.ones((), BF16), jnp.zeros((), BF16))
            return tuple(cs)

        cs = lax.fori_loop(0, qb + 1, count_tile, tuple(jnp.zeros((pack, t), BF16) for _ in range(n_acc)))
        c = (cs[0].astype(F32) + cs[1].astype(F32)) + (cs[2].astype(F32) + cs[3].astype(F32))
        cnt = jnp.sum(c, axis=0, keepdims=True)
        ok = cnt >= float(topk)
        return jnp.where(ok, cand16, thr16), jnp.where(ok, cnt, kept)

    def lo_step(it, state):
        thr, kept = state
        cand = thr + jnp.left_shift(jnp.int32(1), 15 - it)
        cand_b = jnp.broadcast_to(cand, (SUBLANES, t))

        def count_tile(kb, cs):
            cs = list(cs)
            for g in range(t // SUBLANES):
                kk = keyst_ref[kb, g * SUBLANES:(g + 1) * SUBLANES, :]
                cs[g % n_acc] = cs[g % n_acc] + jnp.where(kk >= cand_b, 1, 0)
            return tuple(cs)

        cs = lax.fori_loop(0, qb + 1, count_tile,
                           tuple(jnp.zeros((SUBLANES, t), jnp.int32) for _ in range(n_acc)))
        c = (cs[0] + cs[1]) + (cs[2] + cs[3])
        cnt = jnp.sum(c.astype(F32), axis=0, keepdims=True)
        ok = cnt >= float(topk)
        return jnp.where(ok, cand, thr), jnp.where(ok, cnt, kept)

    assert nkb * t <= 256 * pack * n_acc
    thr16, kept = lax.fori_loop(0, 16, hi_step,
                                (jnp.full((1, t), -(2 ** 15), jnp.int32), jnp.zeros((1, t), F32)))
    thr, kept = lax.fori_loop(0, 16, lo_step, (jnp.left_shift(thr16, 16), kept))
    thr = jnp.maximum(thr, INT_MIN + 1)

    def count_keys(pred):
        def count_tile(kb, cs):
            cs = list(cs)
            for g in range(t // SUBLANES):
                kk = keyst_ref[kb, g * SUBLANES:(g + 1) * SUBLANES, :]
                idx = kb * t + g * SUBLANES + lax.broadcasted_iota(jnp.int32, (SUBLANES, t), 0)
                cs[g % n_acc] = cs[g % n_acc] + jnp.where(pred(kk, idx), 1, 0)
            return tuple(cs)

        cs = lax.fori_loop(0, qb + 1, count_tile,
                           tuple(jnp.zeros((SUBLANES, t), jnp.int32) for _ in range(n_acc)))
        return jnp.sum(((cs[0] + cs[1]) + (cs[2] + cs[3])).astype(F32), axis=0, keepdims=True)

    def break_ties():
        thr_s = jnp.broadcast_to(thr, (SUBLANES, t))
        need = float(topk) - count_keys(lambda kk, idx: kk > thr_s)

        def idx_step(it, last):
            cand = last + jnp.left_shift(jnp.int32(1), idx_bits - 1 - it)
            cand_s = jnp.broadcast_to(cand, (SUBLANES, t))
            below = count_keys(lambda kk, idx: jnp.logical_and(kk == thr_s, idx < cand_s))
            return jnp.where(below < need, cand, last)

        return lax.fori_loop(0, idx_bits, idx_step, jnp.zeros((1, t), jnp.int32))

    idx_bits = max(1, (nkb * t - 1).bit_length())
    tied = jnp.max(jnp.where(kept > float(topk), 1.0, 0.0)) > 0.0
    last = lax.cond(tied, break_ties, lambda: jnp.full((1, t), nkb * t, jnp.int32))

    def per_row(v):
        v = jnp.broadcast_to(v, (LANES, t)).T
        return jnp.concatenate([v] * (t // LANES), axis=1)

    thr_b, last_b = per_row(thr), per_row(last)

    def write_tile(kb, carry):
        key = keys_ref[kb]
        keep = jnp.logical_or(key > thr_b, jnp.logical_and(key == thr_b, kb * t + col0 <= last_b))
        o_ref[kb] = jnp.where(keep, 0.0, NEG).astype(o_ref.dtype)
        return carry

    def fill_tile(kb, carry):
        o_ref[kb] = jnp.full((t, t), NEG, o_ref.dtype)
        return carry

    lax.fori_loop(0, qb + 1, write_tile, 0)
    lax.fori_loop(qb + 1, nkb, fill_tile, 0)


def _dsa_index(qi_hm, kit, wi, *, t, topk):
    heads, s, idim = qi_hm.shape
    nkb = s // t
    return pl.pallas_call(
        functools.partial(_dsa_index_kernel, t=t, topk=topk),
        out_shape=jax.ShapeDtypeStruct((nkb, s, t), BF16),
        grid=(nkb,),
        in_specs=[pl.BlockSpec((heads, t, idim), lambda i: (0, i, 0)),
                  pl.BlockSpec((nkb, idim, t), lambda i: (0, 0, 0)),
                  pl.BlockSpec((t, heads), lambda i: (i, 0))],
        out_specs=pl.BlockSpec((nkb, t, t), lambda i: (0, i, 0)),
        scratch_shapes=[pltpu.VMEM((nkb, t, t), jnp.int32),
                        pltpu.VMEM((nkb, t, t), jnp.int32),
                        pltpu.VMEM((nkb, t, t), BF16),
                        pltpu.VMEM((heads, t, t), F32)],
        compiler_params=_cparams(1),
        name="dsa_index",
    )(qi_hm, kit, wi)


def _dsa_attn_kernel(q_ref, k_ref, v_ref, mask_ref, near_ref, o_ref, s0_ref, s1_ref, *, t, tm):
    qb = pl.program_id(1)
    q = q_ref[...]
    per = t // tm
    nb = t // MAX_DISTANCE

    def add_near(s, which):
        out_rows = []
        for i in range(nb):
            blocks = []
            for j in range(nb):
                blk = s[i * MAX_DISTANCE:(i + 1) * MAX_DISTANCE, j * MAX_DISTANCE:(j + 1) * MAX_DISTANCE]
                if which == "diag" and i == j:
                    blk = blk + near_ref[0, 0]
                elif (which == "diag" and i == j + 1) or (which == "before" and i == 0 and j == nb - 1):
                    blk = blk + near_ref[0, 1]
                blocks.append(blk)
            out_rows.append(jnp.concatenate(blocks, axis=1))
        return jnp.concatenate(out_rows, axis=0)

    bufs = (s0_ref, s1_ref)

    def logits_into(kc, dst_ref):
        rows = pl.ds(pl.multiple_of(kc * t, t), t)
        dst_ref[...] = lax.dot_general(q, k_ref[rows, :], (((1,), (1,)), ((), ())), preferred_element_type=F32)

    def step(kc, which, parity, carry):
        if which != "diag":
            logits_into(kc + 1, bufs[1 - parity])
        m, acc = carry
        s = bufs[parity][...] + jnp.concatenate([mask_ref[kc * per + j] for j in range(per)],
                                                axis=1).astype(F32)
        if which != "far":
            s = add_near(s, which)
        m_new = jnp.maximum(m, jnp.max(s, axis=1, keepdims=True))
        p = jnp.exp2(s - m_new)
        rows = pl.ds(pl.multiple_of(kc * t, t), t)
        acc = jnp.exp2(m - m_new) * acc + jnp.dot(p.astype(BF16), v_ref[rows, :], preferred_element_type=F32)
        return m_new, acc

    def far_pair(i, carry):
        carry = step(2 * i, "far", 0, carry)
        return step(2 * i + 1, "far", 1, carry)

    def tail(kinds):
        def run(k0, carry):
            for n, which in enumerate(kinds):
                carry = step(k0 + n, which, n % 2, carry)
            return carry
        return run

    logits_into(0, s0_ref)
    n_far = jnp.maximum(qb - 1, 0)
    carry = (jnp.full((t, 1), NEG, F32), jnp.zeros((t, 2 * HEAD_DIM), F32))
    carry = lax.fori_loop(0, n_far // 2, far_pair, carry)
    k0 = 2 * (n_far // 2)
    case = jnp.where(qb == 0, 0, 1 + n_far % 2)
    _, acc = lax.switch(case, [tail(("diag",)), tail(("before", "diag")), tail(("far", "before", "diag"))],
                        k0, carry)
    o_ref[...] = (acc[:, :HEAD_DIM] / acc[:, HEAD_DIM:]).astype(o_ref.dtype)


def _dsa_attn(qk, v_ones, mask, near, *, t, heads):
    s = qk.shape[0]
    nkb, _, tm = mask.shape
    return pl.pallas_call(
        functools.partial(_dsa_attn_kernel, t=t, tm=tm),
        out_shape=jax.ShapeDtypeStruct((s, heads * HEAD_DIM), BF16),
        grid=(heads, s // t),
        in_specs=[pl.BlockSpec((t, HEAD_DIM), lambda h, i: (i, h)),
                  pl.BlockSpec((s, HEAD_DIM), lambda h, i: (0, heads + h)),
                  pl.BlockSpec((s, 2 * HEAD_DIM), lambda h, i: (0, h)),
                  pl.BlockSpec((nkb, t, tm), lambda h, i: (0, i, 0)),
                  pl.BlockSpec((1, 2, MAX_DISTANCE, MAX_DISTANCE), lambda h, i: (h, 0, 0, 0))],
        out_specs=pl.BlockSpec((t, HEAD_DIM), lambda h, i: (i, h)),
        scratch_shapes=[pltpu.VMEM((t, t), F32), pltpu.VMEM((t, t), F32)],
        compiler_params=_cparams(2),
        name="dsa_attn",
    )(qk, qk, v_ones, mask, near)


def _sb_kernel(q_ref, k_ref, v_ref, tri_ref, o_ref, *, t, tk):
    qb = pl.program_id(1)
    q = q_ref[...]
    tri = tri_ref[...]
    scale2 = HEAD_DIM ** -0.5 * math.log2(math.e)
    per = t // tk
    small = max(per // 2, 1)

    def chunk(kb0, n, masked, carry):
        before, acc = carry
        width = n * tk
        rows = pl.ds(pl.multiple_of(kb0 * tk, tk), width)
        z = lax.dot_general(q, k_ref[rows, :], (((1,), (1,)), ((), ())), preferred_element_type=F32) * scale2
        lo_z, hi_z = jnp.minimum(z, 0.0), jnp.maximum(z, 0.0)
        lg = jnp.log2(1.0 + jnp.exp2(lo_z - hi_z))
        sp = hi_z + lg
        log_beta = lo_z - lg
        if masked:
            strict = kb0 * tk + lax.broadcasted_iota(jnp.int32, (t, width), 1) < row
            sp = jnp.where(strict, sp, 0.0)
        hi = sp.astype(BF16)
        lo = (sp - hi.astype(F32)).astype(BF16)
        cs = []
        for u in range(n):
            sl = slice(u * tk, (u + 1) * tk)
            cs.append(jnp.dot(jnp.concatenate([hi[:, sl], lo[:, sl]], axis=1), tri, preferred_element_type=F32))
        a = [None] * n
        for u in reversed(range(n)):
            a[u] = jnp.exp2(log_beta[:, u * tk:(u + 1) * tk] - (before + cs[u][:, :tk]))
            before = before + cs[u][:, tk:]
        a = jnp.concatenate(a, axis=1)
        if masked:
            a = jnp.where(strict, a, 0.0)
        acc = acc + jnp.dot(a.astype(BF16), v_ref[rows, :], preferred_element_type=F32)
        return before, acc

    def walk(n_iters, kb0_of, n, state):
        def cond(c):
            return jnp.logical_and(c[0] < n_iters, c[1])

        def body(c):
            before, acc = chunk(kb0_of(c[0]), n, False, c[2:])
            return c[0] + 1, jnp.min(before) < SB_EXIT_BITS, before, acc

        return lax.while_loop(cond, body, (jnp.int32(0),) + state[1:])

    row = qb * t + lax.broadcasted_iota(jnp.int32, (t, t), 0)
    carry = (jnp.zeros((t, tk), F32), jnp.zeros((t, HEAD_DIM), F32))
    before, acc = chunk(qb * per, per, True, carry)
    state = (jnp.int32(0), jnp.min(before) < SB_EXIT_BITS, before, acc)
    n_rest = qb * per
    n_small = jnp.minimum(n_rest, per) // small
    state = walk(n_small, lambda i: n_rest - (i + 1) * small, small, state)
    state = walk(qb - 1, lambda i: n_rest - per - (i + 1) * per, per, state)
    o_ref[...] = state[3].astype(o_ref.dtype)


def _sb_tri(tk):
    j = jnp.arange(tk)
    half = jnp.concatenate([j[:, None] > j[None, :], jnp.ones((tk, tk), bool)], axis=1)
    return jnp.concatenate([half, half], axis=0).astype(BF16)


def _sb_attn(rest, tri, *, t, tk, heads, q_blk, k_blk, v_blk):
    s = rest.shape[0]
    return pl.pallas_call(
        functools.partial(_sb_kernel, t=t, tk=tk),
        out_shape=jax.ShapeDtypeStruct((s, heads * HEAD_DIM), BF16),
        grid=(heads, s // t),
        in_specs=[pl.BlockSpec((t, HEAD_DIM), lambda h, i: (i, q_blk + h)),
                  pl.BlockSpec((s, HEAD_DIM), lambda h, i: (0, k_blk + h)),
                  pl.BlockSpec((s, HEAD_DIM), lambda h, i: (0, v_blk + h)),
                  pl.BlockSpec((2 * tk, 2 * tk), lambda h, i: (0, 0))],
        out_specs=pl.BlockSpec((t, HEAD_DIM), lambda h, i: (i, h)),
        compiler_params=_cparams(2),
        name="sb_attn",
    )(rest, rest, rest, tri)


def _rel_bucket(n):
    max_exact = N_BUCKETS // 2
    nf = jnp.maximum(n, 1).astype(F32)
    large = max_exact + (jnp.log(nf / max_exact) / math.log(MAX_DISTANCE / max_exact)
                         * (N_BUCKETS - max_exact)).astype(jnp.int32)
    large = jnp.minimum(large, N_BUCKETS - 1)
    return jnp.where(n < max_exact, n, large)


def _bias_tables(rel_bias):
    b = MAX_DISTANCE
    by_dist = (rel_bias[_rel_bucket(jnp.arange(2 * b))] - rel_bias[N_BUCKETS - 1]) * math.log2(math.e)
    r = jnp.arange(b)[:, None]
    c = jnp.arange(b)[None, :]
    near = jnp.stack([by_dist[jnp.maximum(r - c, 0)], by_dist[b + r - c]])
    return jnp.transpose(near, (3, 0, 1, 2))


def kernel(x, positions, rel_bias, ffn1_norm, ffn1_gate, ffn1_up, ffn1_down, mix_norm, w_in, pool_w,
           pool_scale, q_norm, k_norm, w_out, ffn2_norm, ffn2_gate, ffn2_up, ffn2_down):
    del positions
    b, s, d = x.shape
    assert b == 1
    depth = w_in.shape[0]
    pool_width = pool_w.shape[1] * pool_w.shape[2]
    dsa_width = (w_in.shape[2] - pool_width - IDX_HEADS * IDX_DIM - IDX_DIM - IDX_HEADS) // 6
    sb_width = dsa_width
    dsa_heads = dsa_width // HEAD_DIM
    sb_heads = sb_width // HEAD_DIM
    topk = min(INDEX_TOPK, s // 4)
    t = _pick(s, 256)
    t_att = _pick(s, 512)
    tk_sb = 128
    t_sb = t_att

    o = [int(c) for c in np.cumsum([0, pool_width, dsa_width, dsa_width, dsa_width, IDX_HEADS * IDX_DIM, IDX_DIM,
                                    IDX_HEADS, sb_width, sb_width, sb_width])]
    w_in_t = jnp.swapaxes(w_in, 1, 2)
    ffn1 = (ffn1_norm, ffn1_gate, ffn1_up, ffn1_down)
    ffn2 = (ffn2_norm, ffn2_gate, ffn2_up, ffn2_down)

    near = _bias_tables(rel_bias)
    tri = _sb_tri(tk_sb)

    def ffn(xin, weights, layer):
        norm, wg, wu, wd = weights
        xn = _rmsnorm(xin, norm[layer])
        act = _matmul(xn, [wg, wu], layer, mode="swiglu", out_dtype=BF16, tm=1024, tn=256, name="ffn_up")
        return _matmul(act, [wd], layer, mode="residual", out_dtype=F32, tm=512, tn=512, extra=xin,
                       res_scale=0.5, name="ffn_down")

    h = x[0]
    for i in range(depth):
        h = ffn(h, ffn1, i)

        hn = _rmsnorm(h, mix_norm[i])
        gain = jnp.concatenate([jnp.tile(q_norm[i] * DSA_LOGIT_SCALE2, dsa_heads),
                                jnp.tile(k_norm[i], dsa_heads)])[None, :]
        proj = functools.partial(_matmul, hn, [w_in_t], i, tm=1024, w_nk=True)
        u_pool = proj(mode="cast", out_dtype=F32, tn=512, col0=o[0], n=pool_width, name="proj_pool")
        qk = proj(mode="headnorm", out_dtype=BF16, tn=512, col0=o[1], n=2 * dsa_width, extra=gain,
                  name="proj_qk")
        v_qi = proj(mode="cast", out_dtype=BF16, tn=512, col0=o[3], n=o[5] - o[3], name="proj_v_qi")
        small = proj(mode="cast", out_dtype=F32, tn=LANES, col0=o[5], n=LANES, name="proj_small")
        qkv_sb = proj(mode="cast", out_dtype=BF16, tn=512, col0=o[7], n=3 * sb_width, name="proj_sb")

        y_pool = _pool_mixer(u_pool, pool_w[i], pool_scale[i])

        qi_hm = jnp.transpose(v_qi[:, dsa_width:].reshape(s, IDX_HEADS, IDX_DIM), (1, 0, 2))
        kit = jnp.transpose(small[:, :IDX_DIM].astype(BF16).reshape(s // t, t, IDX_DIM), (0, 2, 1))
        wi = small[:, IDX_DIM:IDX_DIM + IDX_HEADS]
        mask = _dsa_index(qi_hm, kit, wi, t=t, topk=topk)
        v_dsa = v_qi[:, :dsa_width].reshape(s, dsa_heads, HEAD_DIM)
        v_ones = jnp.concatenate([v_dsa, jnp.ones_like(v_dsa)], axis=2).reshape(s, 2 * dsa_width)
        y_dsa = _dsa_attn(qk, v_ones, mask, near, t=t_att, heads=dsa_heads)

        y_sb = _sb_attn(qkv_sb, tri, t=t_sb, tk=tk_sb, heads=sb_heads, q_blk=0, k_blk=sb_heads,
                        v_blk=2 * sb_heads)

        y = jnp.concatenate([y_pool, y_dsa, y_sb], axis=1)
        h = _matmul(y, [w_out], i, mode="residual", out_dtype=F32, tm=1024, tn=512, extra=h, name="mix_out")

        h = ffn(h, ffn2, i)
    return h[None]
```

```python
import functools
import math

import jax
import jax.numpy as jnp
import numpy as np
from jax import lax
from jax.experimental import pallas as pl
from jax.experimental.pallas import tpu as pltpu

HEAD_DIM = 128
POOL_WINDOWS = (2, 4, 8, 16)
IDX_HEADS = 16
IDX_DIM = 64
INDEX_TOPK = 256
N_BUCKETS = 32
MAX_DISTANCE = 128
EPS = 1e-6

LANES = 128
SUBLANES = 8
POOL_HISTORY = 16
NEG = -1e30
INT_MIN = -(2 ** 31)
F32_MIN_NORMAL = 2.0 ** -126
BF16_MIN_NORMAL_KEY = 0x0080
DSA_LOGIT_SCALE2 = HEAD_DIM ** -0.5 * math.log2(math.e)
SB_EXIT_BITS = 160.0
VMEM_LIMIT = 56 * 1024 * 1024

F32 = jnp.float32
BF16 = jnp.bfloat16


def _cparams(n_grid):
    return pltpu.CompilerParams(dimension_semantics=("arbitrary",) * n_grid,
                                vmem_limit_bytes=VMEM_LIMIT)


def _pick(n, pref):
    t = min(pref, n)
    while n % t:
        t //= 2
    return t


def _rmsnorm_kernel(x_ref, g_ref, o_ref):
    x = x_ref[...]
    ms = jnp.mean(x * x, axis=-1, keepdims=True)
    o_ref[...] = (x * lax.rsqrt(ms + EPS) * g_ref[...]).astype(o_ref.dtype)


def _rmsnorm(x, g):
    s, d = x.shape
    tm = _pick(s, 256)
    return pl.pallas_call(
        _rmsnorm_kernel,
        out_shape=jax.ShapeDtypeStruct((s, d), BF16),
        grid=(s // tm,),
        in_specs=[pl.BlockSpec((tm, d), lambda i: (i, 0)),
                  pl.BlockSpec((1, d), lambda i: (0, 0))],
        out_specs=pl.BlockSpec((tm, d), lambda i: (i, 0)),
        compiler_params=_cparams(1),
        name="rmsnorm",
    )(x, g.reshape(1, d))


def _mm_kernel(a_ref, *refs, n_w, mode, res_scale, w_nk, rows_outer):
    if rows_outer:
        w_refs, extra, o_ref = refs[:n_w], refs[n_w:-1], refs[-1]
        wb_refs = w_refs
    else:
        w_refs, extra, o_ref, wb_refs = refs[:n_w], refs[n_w:-1 - n_w], refs[-1 - n_w], refs[-n_w:]

        @pl.when(pl.program_id(1) == 0)
        def _():
            for w_ref, wb_ref in zip(w_refs, wb_refs):
                wb_ref[...] = w_ref[...].astype(BF16)

    a = a_ref[...]
    contract = (((1,), (1 if w_nk else 0,)), ((), ()))

    def dot(wb_ref):
        return lax.dot_general(a, wb_ref[...].astype(BF16), contract, preferred_element_type=F32)

    if mode == "swiglu":
        g = dot(wb_refs[0])
        u = dot(wb_refs[1])
        o_ref[...] = (g / (1.0 + jnp.exp(-g)) * u).astype(o_ref.dtype)
        return
    acc = dot(wb_refs[0])
    if mode == "cast":
        o_ref[...] = acc.astype(o_ref.dtype)
    elif mode == "residual":
        o_ref[...] = extra[0][...] + res_scale * acc
    elif mode == "headnorm":
        gain = extra[0][...]
        for c in range(acc.shape[1] // HEAD_DIM):
            sl = slice(c * HEAD_DIM, (c + 1) * HEAD_DIM)
            y = acc[:, sl]
            ms = jnp.mean(y * y, axis=-1, keepdims=True)
            o_ref[:, sl] = (y * lax.rsqrt(ms + EPS) * gain[:, sl]).astype(o_ref.dtype)
    else:
        raise ValueError(mode)


def _matmul(a, ws, layer, *, mode, out_dtype, tm, tn, col0=0, n=None, extra=None, res_scale=1.0, w_nk=False,
            rows_outer=False, name):
    m, k = a.shape
    n = ws[0].shape[1 if w_nk else 2] if n is None else n
    tm, tn = _pick(m, tm), _pick(n, tn)

    def at(f):
        return (lambda i, j: f(i, j)) if rows_outer else (lambda j, i: f(i, j))

    in_specs = [pl.BlockSpec((tm, k), at(lambda i, j: (i, 0)))]
    if w_nk:
        assert col0 % SUBLANES == 0
        n_all = ws[0].shape[1]
        assert n_all % SUBLANES == 0
        ws = [w.reshape(-1, k) for w in ws]
        in_specs += [pl.BlockSpec((pl.Element(tn), pl.Element(k)),
                                  at(lambda i, j: (pl.multiple_of(layer * n_all + col0 + j * tn, SUBLANES), 0)))
                     for _ in ws]
    else:
        assert col0 % tn == 0
        in_specs += [pl.BlockSpec((None, k, tn), at(lambda i, j: (layer, 0, col0 // tn + j))) for _ in ws]
    operands = [a] + list(ws)
    if mode == "residual":
        in_specs.append(pl.BlockSpec((tm, tn), at(lambda i, j: (i, j))))
        operands.append(extra)
    elif mode == "headnorm":
        in_specs.append(pl.BlockSpec((1, tn), at(lambda i, j: (0, j))))
        operands.append(extra)
    scratch = [] if rows_outer else [pltpu.VMEM((tn, k) if w_nk else (k, tn), BF16) for _ in ws]
    return pl.pallas_call(
        functools.partial(_mm_kernel, n_w=len(ws), mode=mode, res_scale=res_scale, w_nk=w_nk,
                          rows_outer=rows_outer),
        out_shape=jax.ShapeDtypeStruct((m, n), out_dtype),
        grid=(m // tm, n // tn) if rows_outer else (n // tn, m // tm),
        in_specs=in_specs,
        out_specs=pl.BlockSpec((tm, tn), at(lambda i, j: (i, j))),
        scratch_shapes=scratch,
        compiler_params=_cparams(2),
        name=name,
    )(*operands)


def _pool_kernel(cur_ref, prev_ref, w_ref, scale_ref, o_ref, *, tm, gd):
    i = pl.program_id(0)
    cur = cur_ref[...]
    prev = jnp.where(i > 0, prev_ref[...], 0.0)
    ext = jnp.concatenate([prev, cur], axis=0)
    t = i * tm + lax.broadcasted_iota(jnp.int32, (tm, 1), 0)
    sums = {1: ext}
    w = 1
    while w < max(POOL_WINDOWS):
        sums[2 * w] = sums[w] + pltpu.roll(sums[w], w, axis=0)
        w *= 2
    for g, win in enumerate(POOL_WINDOWS):
        sl = slice(g * gd, (g + 1) * gd)
        cnt = jnp.minimum(t + 1, win).astype(F32)
        d = sums[win][POOL_HISTORY:, sl] / cnt - cur[:, sl]
        y = jnp.dot(d.astype(BF16), w_ref[g], preferred_element_type=F32)
        o_ref[:, sl] = (y * scale_ref[:, sl]).astype(o_ref.dtype)


def _pool_mixer(u_pool, pool_w, pool_scale):
    s, pw = u_pool.shape
    groups, gd, _ = pool_w.shape
    tm = _pick(s, 512)
    hist_blocks = tm // POOL_HISTORY
    return pl.pallas_call(
        functools.partial(_pool_kernel, tm=tm, gd=gd),
        out_shape=jax.ShapeDtypeStruct((s, pw), BF16),
        grid=(s // tm,),
        in_specs=[pl.BlockSpec((tm, pw), lambda i: (i, 0)),
                  pl.BlockSpec((POOL_HISTORY, pw), lambda i: (jnp.maximum(i * hist_blocks - 1, 0), 0)),
                  pl.BlockSpec((groups, gd, gd), lambda i: (0, 0, 0)),
                  pl.BlockSpec((1, pw), lambda i: (0, 0))],
        out_specs=pl.BlockSpec((tm, pw), lambda i: (i, 0)),
        compiler_params=_cparams(1),
        name="pool_mixer",
    )(u_pool, u_pool, pool_w.astype(BF16), pool_scale.reshape(1, pw))


def _dsa_index_kernel(qi_ref, kit_ref, w_ref, o_ref, keys_ref, keyst_ref, hit_ref, wb_ref, *, t, topk):
    qb = pl.program_id(0)
    nkb = o_ref.shape[0]
    w = w_ref[...] * (IDX_DIM ** -0.5 * IDX_HEADS ** -0.5)
    for h in range(IDX_HEADS):
        wb_ref[h] = jnp.broadcast_to(w[:, h:h + 1], (t, t))
    row = qb * t + lax.broadcasted_iota(jnp.int32, (t, t), 0)
    col0 = lax.broadcasted_iota(jnp.int32, (t, t), 1)

    def score_tile(kb, carry):
        kt = kit_ref[kb]
        acc = jnp.zeros((t, t), F32)
        for h in range(IDX_HEADS):
            sc = jnp.dot(qi_ref[h], kt, preferred_element_type=F32)
            acc = acc + jnp.maximum(sc, 0.0) * wb_ref[h]
        acc = jnp.where(jnp.abs(acc) < F32_MIN_NORMAL, 0.0, acc)
        bits = pltpu.bitcast(acc, jnp.int32)
        key = jnp.where(bits < 0, bits ^ 0x7FFFFFFF, bits)
        key = jnp.where(kb * t + col0 <= row, key, INT_MIN)
        keys_ref[kb] = key
        key_t = key.T
        keyst_ref[kb] = key_t
        bits_t = jnp.where(key_t < 0, key_t ^ 0x7FFFFFFF, key_t) & -65536
        hit_ref[kb] = pltpu.bitcast(bits_t, F32).astype(BF16)
        return carry

    lax.fori_loop(0, qb + 1, score_tile, 0)

    n_acc = 4
    pack = 2 * SUBLANES

    def hi_step(it, state):
        thr16, kept = state
        cand16 = thr16 + jnp.left_shift(jnp.int32(1), 15 - it)
        c16 = jnp.where(jnp.logical_and(cand16 > 0, cand16 < BF16_MIN_NORMAL_KEY), BF16_MIN_NORMAL_KEY, cand16)
        c16 = jnp.where(jnp.logical_and(c16 < 0, c16 >= -BF16_MIN_NORMAL_KEY), 0, c16)
        cand_bits = jnp.left_shift(jnp.where(c16 < 0, c16 ^ 0x7FFF, c16), 16)
        cand_b = jnp.broadcast_to(pltpu.bitcast(cand_bits, F32).astype(BF16), (pack, t))

        def count_tile(kb, cs):
            cs = list(cs)
            for g in range(t // pack):
                xh = hit_ref[kb, g * pack:(g + 1) * pack, :]
                cs[g % n_acc] = cs[g % n_acc] + jnp.where(xh >= cand_b, jnp.ones((), BF16), jnp.zeros((), BF16))
            return tuple(cs)

        cs = lax.fori_loop(0, qb + 1, count_tile, tuple(jnp.zeros((pack, t), BF16) for _ in range(n_acc)))
        c = (cs[0].astype(F32) + cs[1].astype(F32)) + (cs[2].astype(F32) + cs[3].astype(F32))
        cnt = jnp.sum(c, axis=0, keepdims=True)
        ok = cnt >= float(topk)
        return jnp.where(ok, cand16, thr16), jnp.where(ok, cnt, kept)

    def lo_step(it, state):
        thr, kept = state
        cand = thr + jnp.left_shift(jnp.int32(1), 15 - it)
        cand_b = jnp.broadcast_to(cand, (SUBLANES, t))

        def count_tile(kb, cs):
            cs = list(cs)
            for g in range(t // SUBLANES):
                kk = keyst_ref[kb, g * SUBLANES:(g + 1) * SUBLANES, :]
                cs[g % n_acc] = cs[g % n_acc] + jnp.where(kk >= cand_b, 1, 0)
            return tuple(cs)

        cs = lax.fori_loop(0, qb + 1, count_tile,
                           tuple(jnp.zeros((SUBLANES, t), jnp.int32) for _ in range(n_acc)))
        c = (cs[0] + cs[1]) + (cs[2] + cs[3])
        cnt = jnp.sum(c.astype(F32), axis=0, keepdims=True)
        ok = cnt >= float(topk)
        return jnp.where(ok, cand, thr), jnp.where(ok, cnt, kept)

    assert nkb * t <= 256 * pack * n_acc
    thr16, kept = lax.fori_loop(0, 16, hi_step,
                                (jnp.full((1, t), -(2 ** 15), jnp.int32), jnp.zeros((1, t), F32)))
    thr, kept = lax.fori_loop(0, 16, lo_step, (jnp.left_shift(thr16, 16), kept))
    thr = jnp.maximum(thr, INT_MIN + 1)

    def count_keys(pred):
        def count_tile(kb, cs):
            cs = list(cs)
            for g in range(t // SUBLANES):
                kk = keyst_ref[kb, g * SUBLANES:(g + 1) * SUBLANES, :]
                idx = kb * t + g * SUBLANES + lax.broadcasted_iota(jnp.int32, (SUBLANES, t), 0)
                cs[g % n_acc] = cs[g % n_acc] + jnp.where(pred(kk, idx), 1, 0)
            return tuple(cs)

        cs = lax.fori_loop(0, qb + 1, count_tile,
                           tuple(jnp.zeros((SUBLANES, t), jnp.int32) for _ in range(n_acc)))
        return jnp.sum(((cs[0] + cs[1]) + (cs[2] + cs[3])).astype(F32), axis=0, keepdims=True)

    def break_ties():
        thr_s = jnp.broadcast_to(thr, (SUBLANES, t))
        need = float(topk) - count_keys(lambda kk, idx: kk > thr_s)

        def idx_step(it, last):
            cand = last + jnp.left_shift(jnp.int32(1), idx_bits - 1 - it)
            cand_s = jnp.broadcast_to(cand, (SUBLANES, t))
            below = count_keys(lambda kk, idx: jnp.logical_and(kk == thr_s, idx < cand_s))
            return jnp.where(below < need, cand, last)

        return lax.fori_loop(0, idx_bits, idx_step, jnp.zeros((1, t), jnp.int32))

    idx_bits = max(1, (nkb * t - 1).bit_length())
    tied = jnp.max(jnp.where(kept > float(topk), 1.0, 0.0)) > 0.0
    last = lax.cond(tied, break_ties, lambda: jnp.full((1, t), nkb * t, jnp.int32))

    def per_row(v):
        v = jnp.broadcast_to(v, (LANES, t)).T
        return jnp.concatenate([v] * (t // LANES), axis=1)

    thr_b, last_b = per_row(thr), per_row(last)

    def write_tile(kb, carry):
        key = keys_ref[kb]
        keep = jnp.logical_or(key > thr_b, jnp.logical_and(key == thr_b, kb * t + col0 <= last_b))
        o_ref[kb] = jnp.where(keep, 0.0, NEG).astype(o_ref.dtype)
        return carry

    def fill_tile(kb, carry):
        o_ref[kb] = jnp.full((t, t), NEG, o_ref.dtype)
        return carry

    lax.fori_loop(0, qb + 1, write_tile, 0)
    lax.fori_loop(qb + 1, nkb, fill_tile, 0)


def _dsa_index(qi_hm, kit, wi, *, t, topk):
    heads, s, idim = qi_hm.shape
    nkb = s // t
    return pl.pallas_call(
        functools.partial(_dsa_index_kernel, t=t, topk=topk),
        out_shape=jax.ShapeDtypeStruct((nkb, s, t), BF16),
        grid=(nkb,),
        in_specs=[pl.BlockSpec((heads, t, idim), lambda i: (0, i, 0)),
                  pl.BlockSpec((nkb, idim, t), lambda i: (0, 0, 0)),
                  pl.BlockSpec((t, heads), lambda i: (i, 0))],
        out_specs=pl.BlockSpec((nkb, t, t), lambda i: (0, i, 0)),
        scratch_shapes=[pltpu.VMEM((nkb, t, t), jnp.int32),
                        pltpu.VMEM((nkb, t, t), jnp.int32),
                        pltpu.VMEM((nkb, t, t), BF16),
                        pltpu.VMEM((heads, t, t), F32)],
        compiler_params=_cparams(1),
        name="dsa_index",
    )(qi_hm, kit, wi)


def _dsa_attn_kernel(q_ref, k_ref, v_ref, mask_ref, near_ref, o_ref, s0_ref, s1_ref, *, t, tm):
    qb = pl.program_id(1)
    q = q_ref[...]
    per = t // tm
    nb = t // MAX_DISTANCE

    def add_near(s, which):
        out_rows = []
        for i in range(nb):
            blocks = []
            for j in range(nb):
                blk = s[i * MAX_DISTANCE:(i + 1) * MAX_DISTANCE, j * MAX_DISTANCE:(j + 1) * MAX_DISTANCE]
                if which == "diag" and i == j:
                    blk = blk + near_ref[0, 0]
                elif (which == "diag" and i == j + 1) or (which == "before" and i == 0 and j == nb - 1):
                    blk = blk + near_ref[0, 1]
                blocks.append(blk)
            out_rows.append(jnp.concatenate(blocks, axis=1))
        return jnp.concatenate(out_rows, axis=0)

    bufs = (s0_ref, s1_ref)

    def logits_into(kc, dst_ref):
        rows = pl.ds(pl.multiple_of(kc * t, t), t)
        dst_ref[...] = lax.dot_general(q, k_ref[rows, :], (((1,), (1,)), ((), ())), preferred_element_type=F32)

    def step(kc, which, parity, carry):
        if which != "diag":
            logits_into(kc + 1, bufs[1 - parity])
        m, acc = carry
        s = bufs[parity][...] + jnp.concatenate([mask_ref[kc * per + j] for j in range(per)],
                                                axis=1).astype(F32)
        if which != "far":
            s = add_near(s, which)
        m_new = jnp.maximum(m, jnp.max(s, axis=1, keepdims=True))
        p = jnp.exp2(s - m_new)
        rows = pl.ds(pl.multiple_of(kc * t, t), t)
        acc = jnp.exp2(m - m_new) * acc + jnp.dot(p.astype(BF16), v_ref[rows, :], preferred_element_type=F32)
        return m_new, acc

    def far_pair(i, carry):
        carry = step(2 * i, "far", 0, carry)
        return step(2 * i + 1, "far", 1, carry)

    def tail(kinds):
        def run(k0, carry):
            for n, which in enumerate(kinds):
                carry = step(k0 + n, which, n % 2, carry)
            return carry
        return run

    logits_into(0, s0_ref)
    n_far = jnp.maximum(qb - 1, 0)
    carry = (jnp.full((t, 1), NEG, F32), jnp.zeros((t, 2 * HEAD_DIM), F32))
    carry = lax.fori_loop(0, n_far // 2, far_pair, carry)
    k0 = 2 * (n_far // 2)
    case = jnp.where(qb == 0, 0, 1 + n_far % 2)
    _, acc = lax.switch(case, [tail(("diag",)), tail(("before", "diag")), tail(("far", "before", "diag"))],
                        k0, carry)
    o_ref[...] = (acc[:, :HEAD_DIM] / acc[:, HEAD_DIM:]).astype(o_ref.dtype)


def _dsa_attn(qk, v_ones, mask, near, *, t, heads):
    s = qk.shape[0]
    nkb, _, tm = mask.shape
    return pl.pallas_call(
        functools.partial(_dsa_attn_kernel, t=t, tm=tm),
        out_shape=jax.ShapeDtypeStruct((s, heads * HEAD_DIM), BF16),
        grid=(heads, s // t),
        in_specs=[pl.BlockSpec((t, HEAD_DIM), lambda h, i: (i, h)),
                  pl.BlockSpec((s, HEAD_DIM), lambda h, i: (0, heads + h)),
                  pl.BlockSpec((s, 2 * HEAD_DIM), lambda h, i: (0, h)),
                  pl.BlockSpec((nkb, t, tm), lambda h, i: (0, i, 0)),
                  pl.BlockSpec((1, 2, MAX_DISTANCE, MAX_DISTANCE), lambda h, i: (h, 0, 0, 0))],
        out_specs=pl.BlockSpec((t, HEAD_DIM), lambda h, i: (i, h)),
        scratch_shapes=[pltpu.VMEM((t, t), F32), pltpu.VMEM((t, t), F32)],
        compiler_params=_cparams(2),
        name="dsa_attn",
    )(qk, qk, v_ones, mask, near)


def _sb_kernel(q_ref, k_ref, v_ref, tri_ref, o_ref, *, t, tk):
    qb = pl.program_id(1)
    q = q_ref[...]
    tri = tri_ref[...]
    scale2 = HEAD_DIM ** -0.5 * math.log2(math.e)
    per = t // tk
    small = max(per // 2, 1)

    def chunk(kb0, n, masked, carry):
        before, acc = carry
        width = n * tk
        rows = pl.ds(pl.multiple_of(kb0 * tk, tk), width)
        z = lax.dot_general(q, k_ref[rows, :], (((1,), (1,)), ((), ())), preferred_element_type=F32) * scale2
        lo_z, hi_z = jnp.minimum(z, 0.0), jnp.maximum(z, 0.0)
        lg = jnp.log2(1.0 + jnp.exp2(lo_z - hi_z))
        sp = hi_z + lg
        log_beta = lo_z - lg
        if masked:
            strict = kb0 * tk + lax.broadcasted_iota(jnp.int32, (t, width), 1) < row
            sp = jnp.where(strict, sp, 0.0)
        hi = sp.astype(BF16)
        lo = (sp - hi.astype(F32)).astype(BF16)
        cs = []
        for u in range(n):
            sl = slice(u * tk, (u + 1) * tk)
            cs.append(jnp.dot(jnp.concatenate([hi[:, sl], lo[:, sl]], axis=1), tri, preferred_element_type=F32))
        a = [None] * n
        for u in reversed(range(n)):
            a[u] = jnp.exp2(log_beta[:, u * tk:(u + 1) * tk] - (before + cs[u][:, :tk]))
            before = before + cs[u][:, tk:]
        a = jnp.concatenate(a, axis=1)
        if masked:
            a = jnp.where(strict, a, 0.0)
        acc = acc + jnp.dot(a.astype(BF16), v_ref[rows, :], preferred_element_type=F32)
        return before, acc

    def walk(n_iters, kb0_of, n, state):
        def cond(c):
            return jnp.logical_and(c[0] < n_iters, c[1])

        def body(c):
            before, acc = chunk(kb0_of(c[0]), n, False, c[2:])
            return c[0] + 1, jnp.min(before) < SB_EXIT_BITS, before, acc

        return lax.while_loop(cond, body, (jnp.int32(0),) + state[1:])

    row = qb * t + lax.broadcasted_iota(jnp.int32, (t, t), 0)
    carry = (jnp.zeros((t, tk), F32), jnp.zeros((t, HEAD_DIM), F32))
    before, acc = chunk(qb * per, per, True, carry)
    state = (jnp.int32(0), jnp.min(before) < SB_EXIT_BITS, before, acc)
    n_rest = qb * per
    n_small = jnp.minimum(n_rest, per) // small
    state = walk(n_small, lambda i: n_rest - (i + 1) * small, small, state)
    state = walk(qb - 1, lambda i: n_rest - per - (i + 1) * per, per, state)
    o_ref[...] = state[3].astype(o_ref.dtype)


def _sb_tri(tk):
    j = jnp.arange(tk)
    half = jnp.concatenate([j[:, None] > j[None, :], jnp.ones((tk, tk), bool)], axis=1)
    return jnp.concatenate([half, half], axis=0).astype(BF16)


def _sb_attn(rest, tri, *, t, tk, heads, q_blk, k_blk, v_blk):
    s = rest.shape[0]
    return pl.pallas_call(
        functools.partial(_sb_kernel, t=t, tk=tk),
        out_shape=jax.ShapeDtypeStruct((s, heads * HEAD_DIM), BF16),
        grid=(heads, s // t),
        in_specs=[pl.BlockSpec((t, HEAD_DIM), lambda h, i: (i, q_blk + h)),
                  pl.BlockSpec((s, HEAD_DIM), lambda h, i: (0, k_blk + h)),
                  pl.BlockSpec((s, HEAD_DIM), lambda h, i: (0, v_blk + h)),
                  pl.BlockSpec((2 * tk, 2 * tk), lambda h, i: (0, 0))],
        out_specs=pl.BlockSpec((t, HEAD_DIM), lambda h, i: (i, h)),
        compiler_params=_cparams(2),
        name="sb_attn",
    )(rest, rest, rest, tri)


def _rel_bucket(n):
    max_exact = N_BUCKETS // 2
    nf = jnp.maximum(n, 1).astype(F32)
    large = max_exact + (jnp.log(nf / max_exact) / math.log(MAX_DISTANCE / max_exact)
                         * (N_BUCKETS - max_exact)).astype(jnp.int32)
    large = jnp.minimum(large, N_BUCKETS - 1)
    return jnp.where(n < max_exact, n, large)


def _bias_tables(rel_bias):
    b = MAX_DISTANCE
    by_dist = (rel_bias[_rel_bucket(jnp.arange(2 * b))] - rel_bias[N_BUCKETS - 1]) * math.log2(math.e)
    r = jnp.arange(b)[:, None]
    c = jnp.arange(b)[None, :]
    near = jnp.stack([by_dist[jnp.maximum(r - c, 0)], by_dist[b + r - c]])
    return jnp.transpose(near, (3, 0, 1, 2))


def kernel(x, positions, rel_bias, ffn1_norm, ffn1_gate, ffn1_up, ffn1_down, mix_norm, w_in, pool_w,
           pool_scale, q_norm, k_norm, w_out, ffn2_norm, ffn2_gate, ffn2_up, ffn2_down):
    del positions
    b, s, d = x.shape
    assert b == 1
    depth = w_in.shape[0]
    pool_width = pool_w.shape[1] * pool_w.shape[2]
    dsa_width = (w_in.shape[2] - pool_width - IDX_HEADS * IDX_DIM - IDX_DIM - IDX_HEADS) // 6
    sb_width = dsa_width
    dsa_heads = dsa_width // HEAD_DIM
    sb_heads = sb_width // HEAD_DIM
    topk = min(INDEX_TOPK, s // 4)
    t = _pick(s, 256)
    t_att = _pick(s, 512)
    tk_sb = 128
    t_sb = t_att

    o = [int(c) for c in np.cumsum([0, pool_width, dsa_width, dsa_width, dsa_width, IDX_HEADS * IDX_DIM, IDX_DIM,
                                    IDX_HEADS, sb_width, sb_width, sb_width])]
    w_in_t = jnp.swapaxes(w_in, 1, 2)
    ffn1 = (ffn1_norm, ffn1_gate, ffn1_up, ffn1_down)
    ffn2 = (ffn2_norm, ffn2_gate, ffn2_up, ffn2_down)

    near = _bias_tables(rel_bias)
    tri = _sb_tri(tk_sb)

    def ffn(xin, weights, layer):
        norm, wg, wu, wd = weights
        xn = _rmsnorm(xin, norm[layer])
        act = _matmul(xn, [wg, wu], layer, mode="swiglu", out_dtype=BF16, tm=1024, tn=256, name="ffn_up")
        return _matmul(act, [wd], layer, mode="residual", out_dtype=F32, tm=1024, tn=256, extra=xin,
                       res_scale=0.5, rows_outer=True, name="ffn_down")

    h = x[0]
    for i in range(depth):
        h = ffn(h, ffn1, i)

        hn = _rmsnorm(h, mix_norm[i])
        gain = jnp.concatenate([jnp.tile(q_norm[i] * DSA_LOGIT_SCALE2, dsa_heads),
                                jnp.tile(k_norm[i], dsa_heads)])[None, :]
        proj = functools.partial(_matmul, hn, [w_in_t], i, tm=1024, w_nk=True)
        u_pool = proj(mode="cast", out_dtype=F32, tn=512, col0=o[0], n=pool_width, name="proj_pool")
        qk = proj(mode="headnorm", out_dtype=BF16, tn=512, col0=o[1], n=2 * dsa_width, extra=gain,
                  name="proj_qk")
        v_qi = proj(mode="cast", out_dtype=BF16, tn=512, col0=o[3], n=o[5] - o[3], name="proj_v_qi")
        small = proj(mode="cast", out_dtype=F32, tn=LANES, col0=o[5], n=LANES, name="proj_small")
        qkv_sb = proj(mode="cast", out_dtype=BF16, tn=512, col0=o[7], n=3 * sb_width, name="proj_sb")

        y_pool = _pool_mixer(u_pool, pool_w[i], pool_scale[i])

        qi_hm = jnp.transpose(v_qi[:, dsa_width:].reshape(s, IDX_HEADS, IDX_DIM), (1, 0, 2))
        kit = jnp.transpose(small[:, :IDX_DIM].astype(BF16).reshape(s // t, t, IDX_DIM), (0, 2, 1))
        wi = small[:, IDX_DIM:IDX_DIM + IDX_HEADS]
        mask = _dsa_index(qi_hm, kit, wi, t=t, topk=topk)
        v_dsa = v_qi[:, :dsa_width].reshape(s, dsa_heads, HEAD_DIM)
        v_ones = jnp.concatenate([v_dsa, jnp.ones_like(v_dsa)], axis=2).reshape(s, 2 * dsa_width)
        y_dsa = _dsa_attn(qk, v_ones, mask, near, t=t_att, heads=dsa_heads)

        y_sb = _sb_attn(qkv_sb, tri, t=t_sb, tk=tk_sb, heads=sb_heads, q_blk=0, k_blk=sb_heads,
                        v_blk=2 * sb_heads)

        y = jnp.concatenate([y_pool, y_dsa, y_sb], axis=1)
        h = _matmul(y, [w_out], i, mode="residual", out_dtype=F32, tm=1024, tn=512, extra=h, name="mix_out")

        h = ffn(h, ffn2, i)
    return h[None]
```

```python
import functools
import math

import jax
import jax.numpy as jnp
import numpy as np
from jax import lax
from jax.experimental import pallas as pl
from jax.experimental.pallas import tpu as pltpu

HEAD_DIM = 128
POOL_WINDOWS = (2, 4, 8, 16)
IDX_HEADS = 16
IDX_DIM = 64
INDEX_TOPK = 256
N_BUCKETS = 32
MAX_DISTANCE = 128
EPS = 1e-6

LANES = 128
SUBLANES = 8
POOL_HISTORY = 16
NEG = -1e30
INT_MIN = -(2 ** 31)
F32_MIN_NORMAL = 2.0 ** -126
BF16_MIN_NORMAL_KEY = 0x0080
DSA_LOGIT_SCALE2 = HEAD_DIM ** -0.5 * math.log2(math.e)
SB_EXIT_BITS = 160.0
VMEM_LIMIT = 56 * 1024 * 1024

F32 = jnp.float32
BF16 = jnp.bfloat16


def _cparams(n_grid):
    return pltpu.CompilerParams(dimension_semantics=("arbitrary",) * n_grid,
                                vmem_limit_bytes=VMEM_LIMIT)


def _pick(n, pref):
    t = min(pref, n)
    while n % t:
        t //= 2
    return t


def _rmsnorm_kernel(x_ref, g_ref, o_ref):
    x = x_ref[...]
    ms = jnp.mean(x * x, axis=-1, keepdims=True)
    o_ref[...] = (x * lax.rsqrt(ms + EPS) * g_ref[...]).astype(o_ref.dtype)


def _rmsnorm(x, g):
    s, d = x.shape
    tm = _pick(s, 256)
    return pl.pallas_call(
        _rmsnorm_kernel,
        out_shape=jax.ShapeDtypeStruct((s, d), BF16),
        grid=(s // tm,),
        in_specs=[pl.BlockSpec((tm, d), lambda i: (i, 0)),
                  pl.BlockSpec((1, d), lambda i: (0, 0))],
        out_specs=pl.BlockSpec((tm, d), lambda i: (i, 0)),
        compiler_params=_cparams(1),
        name="rmsnorm",
    )(x, g.reshape(1, d))


def _mm_kernel(a_ref, *refs, n_w, mode, res_scale, w_nk, rows_outer):
    if rows_outer:
        w_refs, extra, o_ref = refs[:n_w], refs[n_w:-1], refs[-1]
        wb_refs = w_refs
    else:
        w_refs, extra, o_ref, wb_refs = refs[:n_w], refs[n_w:-1 - n_w], refs[-1 - n_w], refs[-n_w:]

        @pl.when(pl.program_id(1) == 0)
        def _():
            for w_ref, wb_ref in zip(w_refs, wb_refs):
                wb_ref[...] = w_ref[...].astype(BF16)

    a = a_ref[...]
    contract = (((1,), (1 if w_nk else 0,)), ((), ()))

    def dot(wb_ref):
        return lax.dot_general(a, wb_ref[...].astype(BF16), contract, preferred_element_type=F32)

    if mode == "swiglu":
        g = dot(wb_refs[0])
        u = dot(wb_refs[1])
        o_ref[...] = (g / (1.0 + jnp.exp(-g)) * u).astype(o_ref.dtype)
        return
    acc = dot(wb_refs[0])
    if mode == "cast":
        o_ref[...] = acc.astype(o_ref.dtype)
    elif mode == "residual":
        o_ref[...] = extra[0][...] + res_scale * acc
    elif mode == "headnorm":
        gain = extra[0][...]
        for c in range(acc.shape[1] // HEAD_DIM):
            sl = slice(c * HEAD_DIM, (c + 1) * HEAD_DIM)
            y = acc[:, sl]
            ms = jnp.mean(y * y, axis=-1, keepdims=True)
            o_ref[:, sl] = (y * lax.rsqrt(ms + EPS) * gain[:, sl]).astype(o_ref.dtype)
    else:
        raise ValueError(mode)


def _matmul(a, ws, layer, *, mode, out_dtype, tm, tn, col0=0, n=None, extra=None, res_scale=1.0, w_nk=False,
            rows_outer=False, name):
    m, k = a.shape
    n = ws[0].shape[1 if w_nk else 2] if n is None else n
    tm, tn = _pick(m, tm), _pick(n, tn)

    def at(f):
        return (lambda i, j: f(i, j)) if rows_outer else (lambda j, i: f(i, j))

    in_specs = [pl.BlockSpec((tm, k), at(lambda i, j: (i, 0)))]
    if w_nk:
        assert col0 % SUBLANES == 0
        n_all = ws[0].shape[1]
        assert n_all % SUBLANES == 0
        ws = [w.reshape(-1, k) for w in ws]
        in_specs += [pl.BlockSpec((pl.Element(tn), pl.Element(k)),
                                  at(lambda i, j: (pl.multiple_of(layer * n_all + col0 + j * tn, SUBLANES), 0)))
                     for _ in ws]
    else:
        assert col0 % tn == 0
        in_specs += [pl.BlockSpec((None, k, tn), at(lambda i, j: (layer, 0, col0 // tn + j))) for _ in ws]
    operands = [a] + list(ws)
    if mode == "residual":
        in_specs.append(pl.BlockSpec((tm, tn), at(lambda i, j: (i, j))))
        operands.append(extra)
    elif mode == "headnorm":
        in_specs.append(pl.BlockSpec((1, tn), at(lambda i, j: (0, j))))
        operands.append(extra)
    scratch = [] if rows_outer else [pltpu.VMEM((tn, k) if w_nk else (k, tn), BF16) for _ in ws]
    return pl.pallas_call(
        functools.partial(_mm_kernel, n_w=len(ws), mode=mode, res_scale=res_scale, w_nk=w_nk,
                          rows_outer=rows_outer),
        out_shape=jax.ShapeDtypeStruct((m, n), out_dtype),
        grid=(m // tm, n // tn) if rows_outer else (n // tn, m // tm),
        in_specs=in_specs,
        out_specs=pl.BlockSpec((tm, tn), at(lambda i, j: (i, j))),
        scratch_shapes=scratch,
        compiler_params=_cparams(2),
        name=name,
    )(*operands)


def _mix_out_kernel(*refs, n_in):
    y_refs, w_refs = refs[:n_in], refs[n_in:2 * n_in]
    res_ref, o_ref, wb_refs = refs[2 * n_in], refs[2 * n_in + 1], refs[2 * n_in + 2:]

    @pl.when(pl.program_id(1) == 0)
    def _():
        for w_ref, wb_ref in zip(w_refs, wb_refs):
            wb_ref[...] = w_ref[...].astype(BF16)

    acc = res_ref[...]
    for y_ref, wb_ref in zip(y_refs, wb_refs):
        acc = acc + jnp.dot(y_ref[...], wb_ref[...], preferred_element_type=F32)
    o_ref[...] = acc


def _mix_out(ys, w_out, layer, res, *, tm, tn):
    m, n = res.shape
    k = w_out.shape[1]
    assert sum(y.shape[1] for y in ys) == k
    tm, tn = _pick(m, tm), _pick(n, tn)
    w2 = w_out.reshape(-1, n)
    row0 = [int(r) for r in np.cumsum([0] + [y.shape[1] for y in ys[:-1]])]
    assert all(r % SUBLANES == 0 for r in row0) and k % SUBLANES == 0

    def w_spec(kp, r):
        return pl.BlockSpec((pl.Element(kp), pl.Element(tn)),
                            lambda j, i: (pl.multiple_of(layer * k + r, SUBLANES), pl.multiple_of(j * tn, LANES)))

    return pl.pallas_call(
        functools.partial(_mix_out_kernel, n_in=len(ys)),
        out_shape=jax.ShapeDtypeStruct((m, n), F32),
        grid=(n // tn, m // tm),
        in_specs=([pl.BlockSpec((tm, y.shape[1]), lambda j, i: (i, 0)) for y in ys]
                  + [w_spec(y.shape[1], r) for y, r in zip(ys, row0)]
                  + [pl.BlockSpec((tm, tn), lambda j, i: (i, j))]),
        out_specs=pl.BlockSpec((tm, tn), lambda j, i: (i, j)),
        scratch_shapes=[pltpu.VMEM((y.shape[1], tn), BF16) for y in ys],
        compiler_params=_cparams(2),
        name="mix_out",
    )(*ys, *([w2] * len(ys)), res)


def _pool_kernel(cur_ref, prev_ref, w_ref, scale_ref, o_ref, *, tm, gd):
    i = pl.program_id(0)
    cur = cur_ref[...]
    prev = jnp.where(i > 0, prev_ref[...], 0.0)
    ext = jnp.concatenate([prev, cur], axis=0)
    t = i * tm + lax.broadcasted_iota(jnp.int32, (tm, 1), 0)
    sums = {1: ext}
    w = 1
    while w < max(POOL_WINDOWS):
        sums[2 * w] = sums[w] + pltpu.roll(sums[w], w, axis=0)
        w *= 2
    for g, win in enumerate(POOL_WINDOWS):
        sl = slice(g * gd, (g + 1) * gd)
        cnt = jnp.minimum(t + 1, win).astype(F32)
        d = sums[win][POOL_HISTORY:, sl] / cnt - cur[:, sl]
        y = jnp.dot(d.astype(BF16), w_ref[g], preferred_element_type=F32)
        o_ref[:, sl] = (y * scale_ref[:, sl]).astype(o_ref.dtype)


def _pool_mixer(u_pool, pool_w, pool_scale):
    s, pw = u_pool.shape
    groups, gd, _ = pool_w.shape
    tm = _pick(s, 512)
    hist_blocks = tm // POOL_HISTORY
    return pl.pallas_call(
        functools.partial(_pool_kernel, tm=tm, gd=gd),
        out_shape=jax.ShapeDtypeStruct((s, pw), BF16),
        grid=(s // tm,),
        in_specs=[pl.BlockSpec((tm, pw), lambda i: (i, 0)),
                  pl.BlockSpec((POOL_HISTORY, pw), lambda i: (jnp.maximum(i * hist_blocks - 1, 0), 0)),
                  pl.BlockSpec((groups, gd, gd), lambda i: (0, 0, 0)),
                  pl.BlockSpec((1, pw), lambda i: (0, 0))],
        out_specs=pl.BlockSpec((tm, pw), lambda i: (i, 0)),
        compiler_params=_cparams(1),
        name="pool_mixer",
    )(u_pool, u_pool, pool_w.astype(BF16), pool_scale.reshape(1, pw))


def _dsa_index_kernel(qi_ref, kit_ref, w_ref, o_ref, keys_ref, keyst_ref, hit_ref, wb_ref, *, t, topk):
    qb = pl.program_id(0)
    nkb = o_ref.shape[0]
    w = w_ref[...] * (IDX_DIM ** -0.5 * IDX_HEADS ** -0.5)
    for h in range(IDX_HEADS):
        wb_ref[h] = jnp.broadcast_to(w[:, h:h + 1], (t, t))
    row = qb * t + lax.broadcasted_iota(jnp.int32, (t, t), 0)
    col0 = lax.broadcasted_iota(jnp.int32, (t, t), 1)

    def score_tile(kb, carry):
        kt = kit_ref[kb]
        acc = jnp.zeros((t, t), F32)
        for h in range(IDX_HEADS):
            sc = jnp.dot(qi_ref[h], kt, preferred_element_type=F32)
            acc = acc + jnp.maximum(sc, 0.0) * wb_ref[h]
        acc = jnp.where(jnp.abs(acc) < F32_MIN_NORMAL, 0.0, acc)
        bits = pltpu.bitcast(acc, jnp.int32)
        key = jnp.where(bits < 0, bits ^ 0x7FFFFFFF, bits)
        key = jnp.where(kb * t + col0 <= row, key, INT_MIN)
        keys_ref[kb] = key
        key_t = key.T
        keyst_ref[kb] = key_t
        bits_t = jnp.where(key_t < 0, key_t ^ 0x7FFFFFFF, key_t) & -65536
        hit_ref[kb] = pltpu.bitcast(bits_t, F32).astype(BF16)
        return carry

    lax.fori_loop(0, qb + 1, score_tile, 0)

    n_acc = 4
    pack = 2 * SUBLANES

    def hi_step(it, state):
        thr16, kept = state
        cand16 = thr16 + jnp.left_shift(jnp.int32(1), 15 - it)
        c16 = jnp.where(jnp.logical_and(cand16 > 0, cand16 < BF16_MIN_NORMAL_KEY), BF16_MIN_NORMAL_KEY, cand16)
        c16 = jnp.where(jnp.logical_and(c16 < 0, c16 >= -BF16_MIN_NORMAL_KEY), 0, c16)
        cand_bits = jnp.left_shift(jnp.where(c16 < 0, c16 ^ 0x7FFF, c16), 16)
        cand_b = jnp.broadcast_to(pltpu.bitcast(cand_bits, F32).astype(BF16), (pack, t))

        def count_tile(kb, cs):
            cs = list(cs)
            for g in range(t // pack):
                xh = hit_ref[kb, g * pack:(g + 1) * pack, :]
                cs[g % n_acc] = cs[g % n_acc] + jnp.where(xh >= cand_b, jnp.ones((), BF16), jnp.zeros((), BF16))
            return tuple(cs)

        cs = lax.fori_loop(0, qb + 1, count_tile, tuple(jnp.zeros((pack, t), BF16) for _ in range(n_acc)))
        c = (cs[0].astype(F32) + cs[1].astype(F32)) + (cs[2].astype(F32) + cs[3].astype(F32))
        cnt = jnp.sum(c, axis=0, keepdims=True)
        ok = cnt >= float(topk)
        return jnp.where(ok, cand16, thr16), jnp.where(ok, cnt, kept)

    def lo_step(it, state):
        thr, kept = state
        cand = thr + jnp.left_shift(jnp.int32(1), 15 - it)
        cand_b = jnp.broadcast_to(cand, (SUBLANES, t))

        def count_tile(kb, cs):
            cs = list(cs)
            for g in range(t // SUBLANES):
                kk = keyst_ref[kb, g * SUBLANES:(g + 1) * SUBLANES, :]
                cs[g % n_acc] = cs[g % n_acc] + jnp.where(kk >= cand_b, 1, 0)
            return tuple(cs)

        cs = lax.fori_loop(0, qb + 1, count_tile,
                           tuple(jnp.zeros((SUBLANES, t), jnp.int32) for _ in range(n_acc)))
        c = (cs[0] + cs[1]) + (cs[2] + cs[3])
        cnt = jnp.sum(c.astype(F32), axis=0, keepdims=True)
        ok = cnt >= float(topk)
        return jnp.where(ok, cand, thr), jnp.where(ok, cnt, kept)

    assert nkb * t <= 256 * pack * n_acc
    thr16, kept = lax.fori_loop(0, 16, hi_step,
                                (jnp.full((1, t), -(2 ** 15), jnp.int32), jnp.zeros((1, t), F32)))
    thr, kept = lax.fori_loop(0, 16, lo_step, (jnp.left_shift(thr16, 16), kept))
    thr = jnp.maximum(thr, INT_MIN + 1)

    def count_keys(pred):
        def count_tile(kb, cs):
            cs = list(cs)
            for g in range(t // SUBLANES):
                kk = keyst_ref[kb, g * SUBLANES:(g + 1) * SUBLANES, :]
                idx = kb * t + g * SUBLANES + lax.broadcasted_iota(jnp.int32, (SUBLANES, t), 0)
                cs[g % n_acc] = cs[g % n_acc] + jnp.where(pred(kk, idx), 1, 0)
            return tuple(cs)

        cs = lax.fori_loop(0, qb + 1, count_tile,
                           tuple(jnp.zeros((SUBLANES, t), jnp.int32) for _ in range(n_acc)))
        return jnp.sum(((cs[0] + cs[1]) + (cs[2] + cs[3])).astype(F32), axis=0, keepdims=True)

    def break_ties():
        thr_s = jnp.broadcast_to(thr, (SUBLANES, t))
        need = float(topk) - count_keys(lambda kk, idx: kk > thr_s)

        def idx_step(it, last):
            cand = last + jnp.left_shift(jnp.int32(1), idx_bits - 1 - it)
            cand_s = jnp.broadcast_to(cand, (SUBLANES, t))
            below = count_keys(lambda kk, idx: jnp.logical_and(kk == thr_s, idx < cand_s))
            return jnp.where(below < need, cand, last)

        return lax.fori_loop(0, idx_bits, idx_step, jnp.zeros((1, t), jnp.int32))

    idx_bits = max(1, (nkb * t - 1).bit_length())
    tied = jnp.max(jnp.where(kept > float(topk), 1.0, 0.0)) > 0.0
    last = lax.cond(tied, break_ties, lambda: jnp.full((1, t), nkb * t, jnp.int32))

    def per_row(v):
        v = jnp.broadcast_to(v, (LANES, t)).T
        return jnp.concatenate([v] * (t // LANES), axis=1)

    thr_b, last_b = per_row(thr), per_row(last)

    def write_tile(kb, carry):
        key = keys_ref[kb]
        keep = jnp.logical_or(key > thr_b, jnp.logical_and(key == thr_b, kb * t + col0 <= last_b))
        o_ref[kb] = jnp.where(keep, 0.0, NEG).astype(o_ref.dtype)
        return carry

    def fill_tile(kb, carry):
        o_ref[kb] = jnp.full((t, t), NEG, o_ref.dtype)
        return carry

    lax.fori_loop(0, qb + 1, write_tile, 0)
    lax.fori_loop(qb + 1, nkb, fill_tile, 0)


def _dsa_index(qi_hm, kit, wi, *, t, topk):
    heads, s, idim = qi_hm.shape
    nkb = s // t
    return pl.pallas_call(
        functools.partial(_dsa_index_kernel, t=t, topk=topk),
        out_shape=jax.ShapeDtypeStruct((nkb, s, t), BF16),
        grid=(nkb,),
        in_specs=[pl.BlockSpec((heads, t, idim), lambda i: (0, i, 0)),
                  pl.BlockSpec((nkb, idim, t), lambda i: (0, 0, 0)),
                  pl.BlockSpec((t, heads), lambda i: (i, 0))],
        out_specs=pl.BlockSpec((nkb, t, t), lambda i: (0, i, 0)),
        scratch_shapes=[pltpu.VMEM((nkb, t, t), jnp.int32),
                        pltpu.VMEM((nkb, t, t), jnp.int32),
                        pltpu.VMEM((nkb, t, t), BF16),
                        pltpu.VMEM((heads, t, t), F32)],
        compiler_params=_cparams(1),
        name="dsa_index",
    )(qi_hm, kit, wi)


def _dsa_attn_kernel(q_ref, k_ref, v_ref, mask_ref, near_ref, o_ref, s0_ref, s1_ref, *, t, tm):
    qb = pl.program_id(1)
    q = q_ref[...]
    per = t // tm
    nb = t // MAX_DISTANCE

    def add_near(s, which):
        out_rows = []
        for i in range(nb):
            blocks = []
            for j in range(nb):
                blk = s[i * MAX_DISTANCE:(i + 1) * MAX_DISTANCE, j * MAX_DISTANCE:(j + 1) * MAX_DISTANCE]
                if which == "diag" and i == j:
                    blk = blk + near_ref[0, 0]
                elif (which == "diag" and i == j + 1) or (which == "before" and i == 0 and j == nb - 1):
                    blk = blk + near_ref[0, 1]
                blocks.append(blk)
            out_rows.append(jnp.concatenate(blocks, axis=1))
        return jnp.concatenate(out_rows, axis=0)

    bufs = (s0_ref, s1_ref)

    def logits_into(kc, dst_ref):
        rows = pl.ds(pl.multiple_of(kc * t, t), t)
        dst_ref[...] = lax.dot_general(q, k_ref[rows, :], (((1,), (1,)), ((), ())), preferred_element_type=F32)

    def step(kc, which, parity, carry):
        if which != "diag":
            logits_into(kc + 1, bufs[1 - parity])
        m, acc = carry
        s = bufs[parity][...] + jnp.concatenate([mask_ref[kc * per + j] for j in range(per)],
                                                axis=1).astype(F32)
        if which != "far":
            s = add_near(s, which)
        m_new = jnp.maximum(m, jnp.max(s, axis=1, keepdims=True))
        p = jnp.exp2(s - m_new)
        rows = pl.ds(pl.multiple_of(kc * t, t), t)
        v = v_ref[rows, :]
        v_ones = jnp.concatenate([v, jnp.ones_like(v)], axis=1)
        acc = jnp.exp2(m - m_new) * acc + jnp.dot(p.astype(BF16), v_ones, preferred_element_type=F32)
        return m_new, acc

    def far_pair(i, carry):
        carry = step(2 * i, "far", 0, carry)
        return step(2 * i + 1, "far", 1, carry)

    def tail(kinds):
        def run(k0, carry):
            for n, which in enumerate(kinds):
                carry = step(k0 + n, which, n % 2, carry)
            return carry
        return run

    logits_into(0, s0_ref)
    n_far = jnp.maximum(qb - 1, 0)
    carry = (jnp.full((t, 1), NEG, F32), jnp.zeros((t, 2 * HEAD_DIM), F32))
    carry = lax.fori_loop(0, n_far // 2, far_pair, carry)
    k0 = 2 * (n_far // 2)
    case = jnp.where(qb == 0, 0, 1 + n_far % 2)
    _, acc = lax.switch(case, [tail(("diag",)), tail(("before", "diag")), tail(("far", "before", "diag"))],
                        k0, carry)
    o_ref[...] = (acc[:, :HEAD_DIM] / acc[:, HEAD_DIM:]).astype(o_ref.dtype)


def _dsa_attn(qk, v, mask, near, *, t, heads):
    s = qk.shape[0]
    nkb, _, tm = mask.shape
    return pl.pallas_call(
        functools.partial(_dsa_attn_kernel, t=t, tm=tm),
        out_shape=jax.ShapeDtypeStruct((s, heads * HEAD_DIM), BF16),
        grid=(heads, s // t),
        in_specs=[pl.BlockSpec((t, HEAD_DIM), lambda h, i: (i, h)),
                  pl.BlockSpec((s, HEAD_DIM), lambda h, i: (0, heads + h)),
                  pl.BlockSpec((s, HEAD_DIM), lambda h, i: (0, h)),
                  pl.BlockSpec((nkb, t, tm), lambda h, i: (0, i, 0)),
                  pl.BlockSpec((1, 2, MAX_DISTANCE, MAX_DISTANCE), lambda h, i: (h, 0, 0, 0))],
        out_specs=pl.BlockSpec((t, HEAD_DIM), lambda h, i: (i, h)),
        scratch_shapes=[pltpu.VMEM((t, t), F32), pltpu.VMEM((t, t), F32)],
        compiler_params=_cparams(2),
        name="dsa_attn",
    )(qk, qk, v, mask, near)


def _sb_kernel(q_ref, k_ref, v_ref, tri_ref, o_ref, *, t, tk):
    qb = pl.program_id(1)
    q = q_ref[...]
    tri = tri_ref[...]
    scale2 = HEAD_DIM ** -0.5 * math.log2(math.e)
    per = t // tk
    small = max(per // 2, 1)

    def chunk(kb0, n, masked, carry):
        before, acc = carry
        width = n * tk
        rows = pl.ds(pl.multiple_of(kb0 * tk, tk), width)
        z = lax.dot_general(q, k_ref[rows, :], (((1,), (1,)), ((), ())), preferred_element_type=F32) * scale2
        lo_z, hi_z = jnp.minimum(z, 0.0), jnp.maximum(z, 0.0)
        lg = jnp.log2(1.0 + jnp.exp2(lo_z - hi_z))
        sp = hi_z + lg
        log_beta = lo_z - lg
        if masked:
            strict = kb0 * tk + lax.broadcasted_iota(jnp.int32, (t, width), 1) < row
            sp = jnp.where(strict, sp, 0.0)
        hi = sp.astype(BF16)
        lo = (sp - hi.astype(F32)).astype(BF16)
        cs = []
        for u in range(n):
            sl = slice(u * tk, (u + 1) * tk)
            cs.append(jnp.dot(jnp.concatenate([hi[:, sl], lo[:, sl]], axis=1), tri, preferred_element_type=F32))
        a = [None] * n
        for u in reversed(range(n)):
            a[u] = jnp.exp2(log_beta[:, u * tk:(u + 1) * tk] - (before + cs[u][:, :tk]))
            before = before + cs[u][:, tk:]
        a = jnp.concatenate(a, axis=1)
        if masked:
            a = jnp.where(strict, a, 0.0)
        acc = acc + jnp.dot(a.astype(BF16), v_ref[rows, :], preferred_element_type=F32)
        return before, acc

    def walk(n_iters, kb0_of, n, state):
        def cond(c):
            return jnp.logical_and(c[0] < n_iters, c[1])

        def body(c):
            before, acc = chunk(kb0_of(c[0]), n, False, c[2:])
            return c[0] + 1, jnp.min(before) < SB_EXIT_BITS, before, acc

        return lax.while_loop(cond, body, (jnp.int32(0),) + state[1:])

    row = qb * t + lax.broadcasted_iota(jnp.int32, (t, t), 0)
    carry = (jnp.zeros((t, tk), F32), jnp.zeros((t, HEAD_DIM), F32))
    before, acc = chunk(qb * per, per, True, carry)
    state = (jnp.int32(0), jnp.min(before) < SB_EXIT_BITS, before, acc)
    n_rest = qb * per
    n_small = jnp.minimum(n_rest, per) // small
    state = walk(n_small, lambda i: n_rest - (i + 1) * small, small, state)
    state = walk(qb - 1, lambda i: n_rest - per - (i + 1) * per, per, state)
    o_ref[...] = state[3].astype(o_ref.dtype)


def _sb_tri(tk):
    j = jnp.arange(tk)
    half = jnp.concatenate([j[:, None] > j[None, :], jnp.ones((tk, tk), bool)], axis=1)
    return jnp.concatenate([half, half], axis=0).astype(BF16)


def _sb_attn(rest, tri, *, t, tk, heads, q_blk, k_blk, v_blk):
    s = rest.shape[0]
    return pl.pallas_call(
        functools.partial(_sb_kernel, t=t, tk=tk),
        out_shape=jax.ShapeDtypeStruct((s, heads * HEAD_DIM), BF16),
        grid=(heads, s // t),
        in_specs=[pl.BlockSpec((t, HEAD_DIM), lambda h, i: (i, q_blk + h)),
                  pl.BlockSpec((s, HEAD_DIM), lambda h, i: (0, k_blk + h)),
                  pl.BlockSpec((s, HEAD_DIM), lambda h, i: (0, v_blk + h)),
                  pl.BlockSpec((2 * tk, 2 * tk), lambda h, i: (0, 0))],
        out_specs=pl.BlockSpec((t, HEAD_DIM), lambda h, i: (i, h)),
        compiler_params=_cparams(2),
        name="sb_attn",
    )(rest, rest, rest, tri)


def _rel_bucket(n):
    max_exact = N_BUCKETS // 2
    nf = jnp.maximum(n, 1).astype(F32)
    large = max_exact + (jnp.log(nf / max_exact) / math.log(MAX_DISTANCE / max_exact)
                         * (N_BUCKETS - max_exact)).astype(jnp.int32)
    large = jnp.minimum(large, N_BUCKETS - 1)
    return jnp.where(n < max_exact, n, large)


def _bias_tables(rel_bias):
    b = MAX_DISTANCE
    by_dist = (rel_bias[_rel_bucket(jnp.arange(2 * b))] - rel_bias[N_BUCKETS - 1]) * math.log2(math.e)
    r = jnp.arange(b)[:, None]
    c = jnp.arange(b)[None, :]
    near = jnp.stack([by_dist[jnp.maximum(r - c, 0)], by_dist[b + r - c]])
    return jnp.transpose(near, (3, 0, 1, 2))


def kernel(x, positions, rel_bias, ffn1_norm, ffn1_gate, ffn1_up, ffn1_down, mix_norm, w_in, pool_w,
           pool_scale, q_norm, k_norm, w_out, ffn2_norm, ffn2_gate, ffn2_up, ffn2_down):
    del positions
    b, s, d = x.shape
    assert b == 1
    depth = w_in.shape[0]
    pool_width = pool_w.shape[1] * pool_w.shape[2]
    dsa_width = (w_in.shape[2] - pool_width - IDX_HEADS * IDX_DIM - IDX_DIM - IDX_HEADS) // 6
    sb_width = dsa_width
    dsa_heads = dsa_width // HEAD_DIM
    sb_heads = sb_width // HEAD_DIM
    topk = min(INDEX_TOPK, s // 4)
    t = _pick(s, 256)
    t_att = _pick(s, 512)
    tk_sb = 128
    t_sb = t_att

    o = [int(c) for c in np.cumsum([0, pool_width, dsa_width, dsa_width, dsa_width, IDX_HEADS * IDX_DIM, IDX_DIM,
                                    IDX_HEADS, sb_width, sb_width, sb_width])]
    w_in_t = jnp.swapaxes(w_in, 1, 2)
    ffn1 = (ffn1_norm, ffn1_gate, ffn1_up, ffn1_down)
    ffn2 = (ffn2_norm, ffn2_gate, ffn2_up, ffn2_down)

    near = _bias_tables(rel_bias)
    tri = _sb_tri(tk_sb)

    def ffn(xin, weights, layer):
        norm, wg, wu, wd = weights
        xn = _rmsnorm(xin, norm[layer])
        act = _matmul(xn, [wg, wu], layer, mode="swiglu", out_dtype=BF16, tm=1024, tn=256, name="ffn_up")
        return _matmul(act, [wd], layer, mode="residual", out_dtype=F32, tm=1024, tn=256, extra=xin,
                       res_scale=0.5, rows_outer=True, name="ffn_down")

    h = x[0]
    for i in range(depth):
        h = ffn(h, ffn1, i)

        hn = _rmsnorm(h, mix_norm[i])
        gain = jnp.concatenate([jnp.tile(q_norm[i] * DSA_LOGIT_SCALE2, dsa_heads),
                                jnp.tile(k_norm[i], dsa_heads)])[None, :]
        proj = functools.partial(_matmul, hn, [w_in_t], i, tm=1024, w_nk=True)
        u_pool = proj(mode="cast", out_dtype=F32, tn=512, col0=o[0], n=pool_width, name="proj_pool")
        qk = proj(mode="headnorm", out_dtype=BF16, tn=512, col0=o[1], n=2 * dsa_width, extra=gain,
                  name="proj_qk")
        v_qi = proj(mode="cast", out_dtype=BF16, tn=512, col0=o[3], n=o[5] - o[3], name="proj_v_qi")
        small = proj(mode="cast", out_dtype=F32, tn=LANES, col0=o[5], n=LANES, name="proj_small")
        qkv_sb = proj(mode="cast", out_dtype=BF16, tn=512, col0=o[7], n=3 * sb_width, name="proj_sb")

        y_pool = _pool_mixer(u_pool, pool_w[i], pool_scale[i])

        qi_hm = jnp.transpose(v_qi[:, dsa_width:].reshape(s, IDX_HEADS, IDX_DIM), (1, 0, 2))
        kit = jnp.transpose(small[:, :IDX_DIM].astype(BF16).reshape(s // t, t, IDX_DIM), (0, 2, 1))
        wi = small[:, IDX_DIM:IDX_DIM + IDX_HEADS]
        mask = _dsa_index(qi_hm, kit, wi, t=t, topk=topk)
        y_dsa = _dsa_attn(qk, v_qi, mask, near, t=t_att, heads=dsa_heads)

        y_sb = _sb_attn(qkv_sb, tri, t=t_sb, tk=tk_sb, heads=sb_heads, q_blk=0, k_blk=sb_heads,
                        v_blk=2 * sb_heads)

        h = _mix_out([y_pool, y_dsa, y_sb], w_out, i, h, tm=1024, tn=512)

        h = ffn(h, ffn2, i)
    return h[None]
```

```python
import functools
import math

import jax
import jax.numpy as jnp
import numpy as np
from jax import lax
from jax.experimental import pallas as pl
from jax.experimental.pallas import tpu as pltpu

HEAD_DIM = 128
POOL_WINDOWS = (2, 4, 8, 16)
IDX_HEADS = 16
IDX_DIM = 64
INDEX_TOPK = 256
N_BUCKETS = 32
MAX_DISTANCE = 128
EPS = 1e-6

LANES = 128
SUBLANES = 8
POOL_HISTORY = 16
NEG = -1e30
INT_MIN = -(2 ** 31)
F32_MIN_NORMAL = 2.0 ** -126
BF16_MIN_NORMAL_KEY = 0x0080
DSA_LOGIT_SCALE2 = HEAD_DIM ** -0.5 * math.log2(math.e)
SB_EXIT_BITS = 160.0
VMEM_LIMIT = 56 * 1024 * 1024

F32 = jnp.float32
BF16 = jnp.bfloat16


def _cparams(n_grid):
    return pltpu.CompilerParams(dimension_semantics=("arbitrary",) * n_grid,
                                vmem_limit_bytes=VMEM_LIMIT)


def _pick(n, pref):
    t = min(pref, n)
    while n % t:
        t //= 2
    return t


def _rmsnorm_kernel(x_ref, g_ref, o_ref):
    x = x_ref[...]
    ms = jnp.mean(x * x, axis=-1, keepdims=True)
    o_ref[...] = (x * lax.rsqrt(ms + EPS) * g_ref[...]).astype(o_ref.dtype)


def _rmsnorm(x, g):
    s, d = x.shape
    tm = _pick(s, 256)
    return pl.pallas_call(
        _rmsnorm_kernel,
        out_shape=jax.ShapeDtypeStruct((s, d), BF16),
        grid=(s // tm,),
        in_specs=[pl.BlockSpec((tm, d), lambda i: (i, 0)),
                  pl.BlockSpec((1, d), lambda i: (0, 0))],
        out_specs=pl.BlockSpec((tm, d), lambda i: (i, 0)),
        compiler_params=_cparams(1),
        name="rmsnorm",
    )(x, g.reshape(1, d))


def _mm_kernel(a_ref, *refs, n_w, mode, res_scale, w_nk, rows_outer):
    if rows_outer:
        w_refs, extra, o_ref = refs[:n_w], refs[n_w:-1], refs[-1]
        wb_refs = w_refs
    else:
        w_refs, extra, o_ref, wb_refs = refs[:n_w], refs[n_w:-1 - n_w], refs[-1 - n_w], refs[-n_w:]

        @pl.when(pl.program_id(1) == 0)
        def _():
            for w_ref, wb_ref in zip(w_refs, wb_refs):
                wb_ref[...] = w_ref[...].astype(BF16)

    a = a_ref[...]
    contract = (((1,), (1 if w_nk else 0,)), ((), ()))

    def dot(wb_ref):
        return lax.dot_general(a, wb_ref[...].astype(BF16), contract, preferred_element_type=F32)

    if mode == "swiglu":
        g = dot(wb_refs[0])
        u = dot(wb_refs[1])
        o_ref[...] = (g / (1.0 + jnp.exp(-g)) * u).astype(o_ref.dtype)
        return
    acc = dot(wb_refs[0])
    if mode == "cast":
        o_ref[...] = acc.astype(o_ref.dtype)
    elif mode == "residual":
        o_ref[...] = extra[0][...] + res_scale * acc
    elif mode == "headnorm":
        gain = extra[0][...]
        for c in range(acc.shape[1] // HEAD_DIM):
            sl = slice(c * HEAD_DIM, (c + 1) * HEAD_DIM)
            y = acc[:, sl]
            ms = jnp.mean(y * y, axis=-1, keepdims=True)
            o_ref[:, sl] = (y * lax.rsqrt(ms + EPS) * gain[:, sl]).astype(o_ref.dtype)
    else:
        raise ValueError(mode)


def _matmul(a, ws, layer, *, mode, out_dtype, tm, tn, col0=0, n=None, extra=None, res_scale=1.0, w_nk=False,
            rows_outer=False, name):
    m, k = a.shape
    n = ws[0].shape[1 if w_nk else 2] if n is None else n
    tm, tn = _pick(m, tm), _pick(n, tn)

    def at(f):
        return (lambda i, j: f(i, j)) if rows_outer else (lambda j, i: f(i, j))

    in_specs = [pl.BlockSpec((tm, k), at(lambda i, j: (i, 0)))]
    if w_nk:
        assert col0 % SUBLANES == 0
        n_all = ws[0].shape[1]
        assert n_all % SUBLANES == 0
        ws = [w.reshape(-1, k) for w in ws]
        in_specs += [pl.BlockSpec((pl.Element(tn), pl.Element(k)),
                                  at(lambda i, j: (pl.multiple_of(layer * n_all + col0 + j * tn, SUBLANES), 0)))
                     for _ in ws]
    else:
        assert col0 % tn == 0
        in_specs += [pl.BlockSpec((None, k, tn), at(lambda i, j: (layer, 0, col0 // tn + j))) for _ in ws]
    operands = [a] + list(ws)
    if mode == "residual":
        in_specs.append(pl.BlockSpec((tm, tn), at(lambda i, j: (i, j))))
        operands.append(extra)
    elif mode == "headnorm":
        in_specs.append(pl.BlockSpec((1, tn), at(lambda i, j: (0, j))))
        operands.append(extra)
    scratch = [] if rows_outer else [pltpu.VMEM((tn, k) if w_nk else (k, tn), BF16) for _ in ws]
    return pl.pallas_call(
        functools.partial(_mm_kernel, n_w=len(ws), mode=mode, res_scale=res_scale, w_nk=w_nk,
                          rows_outer=rows_outer),
        out_shape=jax.ShapeDtypeStruct((m, n), out_dtype),
        grid=(m // tm, n // tn) if rows_outer else (n // tn, m // tm),
        in_specs=in_specs,
        out_specs=pl.BlockSpec((tm, tn), at(lambda i, j: (i, j))),
        scratch_shapes=scratch,
        compiler_params=_cparams(2),
        name=name,
    )(*operands)


def _mix_out_kernel(*refs, n_in):
    y_refs, w_refs = refs[:n_in], refs[n_in:2 * n_in]
    res_ref, o_ref, wb_refs = refs[2 * n_in], refs[2 * n_in + 1], refs[2 * n_in + 2:]

    @pl.when(pl.program_id(1) == 0)
    def _():
        for w_ref, wb_ref in zip(w_refs, wb_refs):
            wb_ref[...] = w_ref[...].astype(BF16)

    acc = res_ref[...]
    for y_ref, wb_ref in zip(y_refs, wb_refs):
        acc = acc + jnp.dot(y_ref[...], wb_ref[...], preferred_element_type=F32)
    o_ref[...] = acc


def _mix_out(ys, w_out, layer, res, *, tm, tn):
    m, n = res.shape
    k = w_out.shape[1]
    assert sum(y.shape[1] for y in ys) == k
    tm, tn = _pick(m, tm), _pick(n, tn)
    w2 = w_out.reshape(-1, n)
    row0 = [int(r) for r in np.cumsum([0] + [y.shape[1] for y in ys[:-1]])]
    assert all(r % SUBLANES == 0 for r in row0) and k % SUBLANES == 0

    def w_spec(kp, r):
        return pl.BlockSpec((pl.Element(kp), pl.Element(tn)),
                            lambda j, i: (pl.multiple_of(layer * k + r, SUBLANES), pl.multiple_of(j * tn, LANES)))

    return pl.pallas_call(
        functools.partial(_mix_out_kernel, n_in=len(ys)),
        out_shape=jax.ShapeDtypeStruct((m, n), F32),
        grid=(n // tn, m // tm),
        in_specs=([pl.BlockSpec((tm, y.shape[1]), lambda j, i: (i, 0)) for y in ys]
                  + [w_spec(y.shape[1], r) for y, r in zip(ys, row0)]
                  + [pl.BlockSpec((tm, tn), lambda j, i: (i, j))]),
        out_specs=pl.BlockSpec((tm, tn), lambda j, i: (i, j)),
        scratch_shapes=[pltpu.VMEM((y.shape[1], tn), BF16) for y in ys],
        compiler_params=_cparams(2),
        name="mix_out",
    )(*ys, *([w2] * len(ys)), res)


def _pool_kernel(cur_ref, prev_ref, w_ref, scale_ref, o_ref, *, tm, gd):
    i = pl.program_id(0)
    cur = cur_ref[...]
    prev = jnp.where(i > 0, prev_ref[...], 0.0)
    ext = jnp.concatenate([prev, cur], axis=0)
    t = i * tm + lax.broadcasted_iota(jnp.int32, (tm, 1), 0)
    sums = {1: ext}
    w = 1
    while w < max(POOL_WINDOWS):
        sums[2 * w] = sums[w] + pltpu.roll(sums[w], w, axis=0)
        w *= 2
    for g, win in enumerate(POOL_WINDOWS):
        sl = slice(g * gd, (g + 1) * gd)
        cnt = jnp.minimum(t + 1, win).astype(F32)
        d = sums[win][POOL_HISTORY:, sl] / cnt - cur[:, sl]
        y = jnp.dot(d.astype(BF16), w_ref[g], preferred_element_type=F32)
        o_ref[:, sl] = (y * scale_ref[:, sl]).astype(o_ref.dtype)


def _pool_mixer(u_pool, pool_w, pool_scale):
    s, pw = u_pool.shape
    groups, gd, _ = pool_w.shape
    tm = _pick(s, 512)
    hist_blocks = tm // POOL_HISTORY
    return pl.pallas_call(
        functools.partial(_pool_kernel, tm=tm, gd=gd),
        out_shape=jax.ShapeDtypeStruct((s, pw), BF16),
        grid=(s // tm,),
        in_specs=[pl.BlockSpec((tm, pw), lambda i: (i, 0)),
                  pl.BlockSpec((POOL_HISTORY, pw), lambda i: (jnp.maximum(i * hist_blocks - 1, 0), 0)),
                  pl.BlockSpec((groups, gd, gd), lambda i: (0, 0, 0)),
                  pl.BlockSpec((1, pw), lambda i: (0, 0))],
        out_specs=pl.BlockSpec((tm, pw), lambda i: (i, 0)),
        compiler_params=_cparams(1),
        name="pool_mixer",
    )(u_pool, u_pool, pool_w.astype(BF16), pool_scale.reshape(1, pw))


def _dsa_index_kernel(qi_ref, kit_ref, w_ref, o_ref, keys_ref, keyst_ref, hit_ref, wb_ref, *, t, topk):
    qb = pl.program_id(0)
    nkb = o_ref.shape[0]
    w = w_ref[...] * (IDX_DIM ** -0.5 * IDX_HEADS ** -0.5)
    for h in range(IDX_HEADS):
        wb_ref[h] = jnp.broadcast_to(w[:, h:h + 1], (t, t))
    row = qb * t + lax.broadcasted_iota(jnp.int32, (t, t), 0)
    col0 = lax.broadcasted_iota(jnp.int32, (t, t), 1)

    def scores(kb):
        kt = kit_ref[kb]
        acc = jnp.zeros((t, t), F32)
        for h in range(IDX_HEADS):
            sc = jnp.dot(qi_ref[h], kt, preferred_element_type=F32)
            acc = acc + jnp.maximum(sc, 0.0) * wb_ref[h]
        return acc

    def store_keys(kb, acc):
        acc = jnp.where(jnp.abs(acc) < F32_MIN_NORMAL, 0.0, acc)
        bits = pltpu.bitcast(acc, jnp.int32)
        key = jnp.where(bits < 0, bits ^ 0x7FFFFFFF, bits)
        key = jnp.where(kb * t + col0 <= row, key, INT_MIN)
        keys_ref[kb] = key
        key_t = key.T
        keyst_ref[kb] = key_t
        bits_t = jnp.where(key_t < 0, key_t ^ 0x7FFFFFFF, key_t) & -65536
        hit_ref[kb] = pltpu.bitcast(bits_t, F32).astype(BF16)

    def score_pair(i, carry):
        acc0, acc1 = scores(2 * i), scores(2 * i + 1)
        store_keys(2 * i, acc0)
        store_keys(2 * i + 1, acc1)
        return carry

    def score_tile(kb, carry):
        store_keys(kb, scores(kb))
        return carry

    lax.fori_loop(0, (qb + 1) // 2, score_pair, 0)
    lax.fori_loop(2 * ((qb + 1) // 2), qb + 1, score_tile, 0)

    n_acc = 4
    pack = 2 * SUBLANES

    def hi_step(it, state):
        thr16, kept = state
        cand16 = thr16 + jnp.left_shift(jnp.int32(1), 15 - it)
        c16 = jnp.where(jnp.logical_and(cand16 > 0, cand16 < BF16_MIN_NORMAL_KEY), BF16_MIN_NORMAL_KEY, cand16)
        c16 = jnp.where(jnp.logical_and(c16 < 0, c16 >= -BF16_MIN_NORMAL_KEY), 0, c16)
        cand_bits = jnp.left_shift(jnp.where(c16 < 0, c16 ^ 0x7FFF, c16), 16)
        cand_b = jnp.broadcast_to(pltpu.bitcast(cand_bits, F32).astype(BF16), (pack, t))

        def count_tile(kb, cs):
            cs = list(cs)
            for g in range(t // pack):
                xh = hit_ref[kb, g * pack:(g + 1) * pack, :]
                cs[g % n_acc] = cs[g % n_acc] + jnp.where(xh >= cand_b, jnp.ones((), BF16), jnp.zeros((), BF16))
            return tuple(cs)

        cs = lax.fori_loop(0, qb + 1, count_tile, tuple(jnp.zeros((pack, t), BF16) for _ in range(n_acc)))
        c = (cs[0].astype(F32) + cs[1].astype(F32)) + (cs[2].astype(F32) + cs[3].astype(F32))
        cnt = jnp.sum(c, axis=0, keepdims=True)
        ok = cnt >= float(topk)
        return jnp.where(ok, cand16, thr16), jnp.where(ok, cnt, kept)

    def lo_step(it, state):
        thr, kept = state
        cand = thr + jnp.left_shift(jnp.int32(1), 15 - it)
        cand_b = jnp.broadcast_to(cand, (SUBLANES, t))

        def count_tile(kb, cs):
            cs = list(cs)
            for g in range(t // SUBLANES):
                kk = keyst_ref[kb, g * SUBLANES:(g + 1) * SUBLANES, :]
                cs[g % n_acc] = cs[g % n_acc] + jnp.where(kk >= cand_b, 1, 0)
            return tuple(cs)

        cs = lax.fori_loop(0, qb + 1, count_tile,
                           tuple(jnp.zeros((SUBLANES, t), jnp.int32) for _ in range(n_acc)))
        c = (cs[0] + cs[1]) + (cs[2] + cs[3])
        cnt = jnp.sum(c.astype(F32), axis=0, keepdims=True)
        ok = cnt >= float(topk)
        return jnp.where(ok, cand, thr), jnp.where(ok, cnt, kept)

    assert nkb * t <= 256 * pack * n_acc
    thr16, kept = lax.fori_loop(0, 16, hi_step,
                                (jnp.full((1, t), -(2 ** 15), jnp.int32), jnp.zeros((1, t), F32)))
    thr, kept = lax.fori_loop(0, 16, lo_step, (jnp.left_shift(thr16, 16), kept))
    thr = jnp.maximum(thr, INT_MIN + 1)

    def count_keys(pred):
        def count_tile(kb, cs):
            cs = list(cs)
            for g in range(t // SUBLANES):
                kk = keyst_ref[kb, g * SUBLANES:(g + 1) * SUBLANES, :]
                idx = kb * t + g * SUBLANES + lax.broadcasted_iota(jnp.int32, (SUBLANES, t), 0)
                cs[g % n_acc] = cs[g % n_acc] + jnp.where(pred(kk, idx), 1, 0)
            return tuple(cs)

        cs = lax.fori_loop(0, qb + 1, count_tile,
                           tuple(jnp.zeros((SUBLANES, t), jnp.int32) for _ in range(n_acc)))
        return jnp.sum(((cs[0] + cs[1]) + (cs[2] + cs[3])).astype(F32), axis=0, keepdims=True)

    def break_ties():
        thr_s = jnp.broadcast_to(thr, (SUBLANES, t))
        need = float(topk) - count_keys(lambda kk, idx: kk > thr_s)

        def idx_step(it, last):
            cand = last + jnp.left_shift(jnp.int32(1), idx_bits - 1 - it)
            cand_s = jnp.broadcast_to(cand, (SUBLANES, t))
            below = count_keys(lambda kk, idx: jnp.logical_and(kk == thr_s, idx < cand_s))
            return jnp.where(below < need, cand, last)

        return lax.fori_loop(0, idx_bits, idx_step, jnp.zeros((1, t), jnp.int32))

    idx_bits = max(1, (nkb * t - 1).bit_length())
    tied = jnp.max(jnp.where(kept > float(topk), 1.0, 0.0)) > 0.0
    last = lax.cond(tied, break_ties, lambda: jnp.full((1, t), nkb * t, jnp.int32))

    def per_row(v):
        v = jnp.broadcast_to(v, (LANES, t)).T
        return jnp.concatenate([v] * (t // LANES), axis=1)

    thr_b, last_b = per_row(thr), per_row(last)

    def write_tile(kb, carry):
        key = keys_ref[kb]
        keep = jnp.logical_or(key > thr_b, jnp.logical_and(key == thr_b, kb * t + col0 <= last_b))
        o_ref[kb] = jnp.where(keep, 0.0, NEG).astype(o_ref.dtype)
        return carry

    def fill_tile(kb, carry):
        o_ref[kb] = jnp.full((t, t), NEG, o_ref.dtype)
        return carry

    lax.fori_loop(0, qb + 1, write_tile, 0)
    lax.fori_loop(qb + 1, nkb, fill_tile, 0)


def _dsa_index(qi_hm, kit, wi, *, t, topk):
    heads, s, idim = qi_hm.shape
    nkb = s // t
    return pl.pallas_call(
        functools.partial(_dsa_index_kernel, t=t, topk=topk),
        out_shape=jax.ShapeDtypeStruct((nkb, s, t), BF16),
        grid=(nkb,),
        in_specs=[pl.BlockSpec((heads, t, idim), lambda i: (0, i, 0)),
                  pl.BlockSpec((nkb, idim, t), lambda i: (0, 0, 0)),
                  pl.BlockSpec((t, heads), lambda i: (i, 0))],
        out_specs=pl.BlockSpec((nkb, t, t), lambda i: (0, i, 0)),
        scratch_shapes=[pltpu.VMEM((nkb, t, t), jnp.int32),
                        pltpu.VMEM((nkb, t, t), jnp.int32),
                        pltpu.VMEM((nkb, t, t), BF16),
                        pltpu.VMEM((heads, t, t), F32)],
        compiler_params=_cparams(1),
        name="dsa_index",
    )(qi_hm, kit, wi)


def _dsa_attn_kernel(q_ref, k_ref, v_ref, mask_ref, near_ref, o_ref, s0_ref, s1_ref, *, t, tm):
    qb = pl.program_id(1)
    q = q_ref[...]
    per = t // tm
    nb = t // MAX_DISTANCE

    def add_near(s, which):
        out_rows = []
        for i in range(nb):
            blocks = []
            for j in range(nb):
                blk = s[i * MAX_DISTANCE:(i + 1) * MAX_DISTANCE, j * MAX_DISTANCE:(j + 1) * MAX_DISTANCE]
                if which == "diag" and i == j:
                    blk = blk + near_ref[0, 0]
                elif (which == "diag" and i == j + 1) or (which == "before" and i == 0 and j == nb - 1):
                    blk = blk + near_ref[0, 1]
                blocks.append(blk)
            out_rows.append(jnp.concatenate(blocks, axis=1))
        return jnp.concatenate(out_rows, axis=0)

    bufs = (s0_ref, s1_ref)

    def logits_into(kc, dst_ref):
        rows = pl.ds(pl.multiple_of(kc * t, t), t)
        dst_ref[...] = lax.dot_general(q, k_ref[rows, :], (((1,), (1,)), ((), ())), preferred_element_type=F32)

    def step(kc, which, parity, carry):
        if which != "diag":
            logits_into(kc + 1, bufs[1 - parity])
        m, acc = carry
        s = bufs[parity][...] + jnp.concatenate([mask_ref[kc * per + j] for j in range(per)],
                                                axis=1).astype(F32)
        if which != "far":
            s = add_near(s, which)
        m_new = jnp.maximum(m, jnp.max(s, axis=1, keepdims=True))
        p = jnp.exp2(s - m_new)
        rows = pl.ds(pl.multiple_of(kc * t, t), t)
        v = v_ref[rows, :]
        v_ones = jnp.concatenate([v, jnp.ones_like(v)], axis=1)
        acc = jnp.exp2(m - m_new) * acc + jnp.dot(p.astype(BF16), v_ones, preferred_element_type=F32)
        return m_new, acc

    def far_pair(i, carry):
        carry = step(2 * i, "far", 0, carry)
        return step(2 * i + 1, "far", 1, carry)

    def tail(kinds):
        def run(k0, carry):
            for n, which in enumerate(kinds):
                carry = step(k0 + n, which, n % 2, carry)
            return carry
        return run

    logits_into(0, s0_ref)
    n_far = jnp.maximum(qb - 1, 0)
    carry = (jnp.full((t, 1), NEG, F32), jnp.zeros((t, 2 * HEAD_DIM), F32))
    carry = lax.fori_loop(0, n_far // 2, far_pair, carry)
    k0 = 2 * (n_far // 2)
    case = jnp.where(qb == 0, 0, 1 + n_far % 2)
    _, acc = lax.switch(case, [tail(("diag",)), tail(("before", "diag")), tail(("far", "before", "diag"))],
                        k0, carry)
    o_ref[...] = (acc[:, :HEAD_DIM] / acc[:, HEAD_DIM:]).astype(o_ref.dtype)


def _dsa_attn(qk, v, mask, near, *, t, heads):
    s = qk.shape[0]
    nkb, _, tm = mask.shape
    return pl.pallas_call(
        functools.partial(_dsa_attn_kernel, t=t, tm=tm),
        out_shape=jax.ShapeDtypeStruct((s, heads * HEAD_DIM), BF16),
        grid=(heads, s // t),
        in_specs=[pl.BlockSpec((t, HEAD_DIM), lambda h, i: (i, h)),
                  pl.BlockSpec((s, HEAD_DIM), lambda h, i: (0, heads + h)),
                  pl.BlockSpec((s, HEAD_DIM), lambda h, i: (0, h)),
                  pl.BlockSpec((nkb, t, tm), lambda h, i: (0, i, 0)),
                  pl.BlockSpec((1, 2, MAX_DISTANCE, MAX_DISTANCE), lambda h, i: (h, 0, 0, 0))],
        out_specs=pl.BlockSpec((t, HEAD_DIM), lambda h, i: (i, h)),
        scratch_shapes=[pltpu.VMEM((t, t), F32), pltpu.VMEM((t, t), F32)],
        compiler_params=_cparams(2),
        name="dsa_attn",
    )(qk, qk, v, mask, near)


def _sb_kernel(q_ref, k_ref, v_ref, tri_ref, o_ref, *, t, tk):
    qb = pl.program_id(1)
    q = q_ref[...]
    tri = tri_ref[...]
    scale2 = HEAD_DIM ** -0.5 * math.log2(math.e)
    per = t // tk
    small = max(per // 2, 1)

    def chunk(kb0, n, masked, carry):
        before, acc = carry
        width = n * tk
        rows = pl.ds(pl.multiple_of(kb0 * tk, tk), width)
        z = lax.dot_general(q, k_ref[rows, :], (((1,), (1,)), ((), ())), preferred_element_type=F32) * scale2
        lo_z, hi_z = jnp.minimum(z, 0.0), jnp.maximum(z, 0.0)
        lg = jnp.log2(1.0 + jnp.exp2(lo_z - hi_z))
        sp = hi_z + lg
        log_beta = lo_z - lg
        if masked:
            strict = kb0 * tk + lax.broadcasted_iota(jnp.int32, (t, width), 1) < row
            sp = jnp.where(strict, sp, 0.0)
        hi = sp.astype(BF16)
        lo = (sp - hi.astype(F32)).astype(BF16)
        cs = []
        for u in range(n):
            sl = slice(u * tk, (u + 1) * tk)
            cs.append(jnp.dot(jnp.concatenate([hi[:, sl], lo[:, sl]], axis=1), tri, preferred_element_type=F32))
        a = [None] * n
        for u in reversed(range(n)):
            a[u] = jnp.exp2(log_beta[:, u * tk:(u + 1) * tk] - (before + cs[u][:, :tk]))
            before = before + cs[u][:, tk:]
        a = jnp.concatenate(a, axis=1)
        if masked:
            a = jnp.where(strict, a, 0.0)
        acc = acc + jnp.dot(a.astype(BF16), v_ref[rows, :], preferred_element_type=F32)
        return before, acc

    def walk(n_iters, kb0_of, n, state):
        def cond(c):
            return jnp.logical_and(c[0] < n_iters, c[1])

        def body(c):
            before, acc = chunk(kb0_of(c[0]), n, False, c[2:])
            return c[0] + 1, jnp.min(before) < SB_EXIT_BITS, before, acc

        return lax.while_loop(cond, body, (jnp.int32(0),) + state[1:])

    row = qb * t + lax.broadcasted_iota(jnp.int32, (t, t), 0)
    carry = (jnp.zeros((t, tk), F32), jnp.zeros((t, HEAD_DIM), F32))
    before, acc = chunk(qb * per, per, True, carry)
    state = (jnp.int32(0), jnp.min(before) < SB_EXIT_BITS, before, acc)
    n_rest = qb * per
    n_small = jnp.minimum(n_rest, per) // small
    state = walk(n_small, lambda i: n_rest - (i + 1) * small, small, state)
    state = walk(qb - 1, lambda i: n_rest - per - (i + 1) * per, per, state)
    o_ref[...] = state[3].astype(o_ref.dtype)


def _sb_tri(tk):
    j = jnp.arange(tk)
    half = jnp.concatenate([j[:, None] > j[None, :], jnp.ones((tk, tk), bool)], axis=1)
    return jnp.concatenate([half, half], axis=0).astype(BF16)


def _sb_attn(rest, tri, *, t, tk, heads, q_blk, k_blk, v_blk):
    s = rest.shape[0]
    return pl.pallas_call(
        functools.partial(_sb_kernel, t=t, tk=tk),
        out_shape=jax.ShapeDtypeStruct((s, heads * HEAD_DIM), BF16),
        grid=(heads, s // t),
        in_specs=[pl.BlockSpec((t, HEAD_DIM), lambda h, i: (i, q_blk + h)),
                  pl.BlockSpec((s, HEAD_DIM), lambda h, i: (0, k_blk + h)),
                  pl.BlockSpec((s, HEAD_DIM), lambda h, i: (0, v_blk + h)),
                  pl.BlockSpec((2 * tk, 2 * tk), lambda h, i: (0, 0))],
        out_specs=pl.BlockSpec((t, HEAD_DIM), lambda h, i: (i, h)),
        compiler_params=_cparams(2),
        name="sb_attn",
    )(rest, rest, rest, tri)


def _rel_bucket(n):
    max_exact = N_BUCKETS // 2
    nf = jnp.maximum(n, 1).astype(F32)
    large = max_exact + (jnp.log(nf / max_exact) / math.log(MAX_DISTANCE / max_exact)
                         * (N_BUCKETS - max_exact)).astype(jnp.int32)
    large = jnp.minimum(large, N_BUCKETS - 1)
    return jnp.where(n < max_exact, n, large)


def _bias_tables(rel_bias):
    b = MAX_DISTANCE
    by_dist = (rel_bias[_rel_bucket(jnp.arange(2 * b))] - rel_bias[N_BUCKETS - 1]) * math.log2(math.e)
    r = jnp.arange(b)[:, None]
    c = jnp.arange(b)[None, :]
    near = jnp.stack([by_dist[jnp.maximum(r - c, 0)], by_dist[b + r - c]])
    return jnp.transpose(near, (3, 0, 1, 2))


def kernel(x, positions, rel_bias, ffn1_norm, ffn1_gate, ffn1_up, ffn1_down, mix_norm, w_in, pool_w,
           pool_scale, q_norm, k_norm, w_out, ffn2_norm, ffn2_gate, ffn2_up, ffn2_down):
    del positions
    b, s, d = x.shape
    assert b == 1
    depth = w_in.shape[0]
    pool_width = pool_w.shape[1] * pool_w.shape[2]
    dsa_width = (w_in.shape[2] - pool_width - IDX_HEADS * IDX_DIM - IDX_DIM - IDX_HEADS) // 6
    sb_width = dsa_width
    dsa_heads = dsa_width // HEAD_DIM
    sb_heads = sb_width // HEAD_DIM
    topk = min(INDEX_TOPK, s // 4)
    t = _pick(s, 256)
    t_att = _pick(s, 512)
    tk_sb = 128
    t_sb = t_att

    o = [int(c) for c in np.cumsum([0, pool_width, dsa_width, dsa_width, dsa_width, IDX_HEADS * IDX_DIM, IDX_DIM,
                                    IDX_HEADS, sb_width, sb_width, sb_width])]
    w_in_t = jnp.swapaxes(w_in, 1, 2)
    ffn1 = (ffn1_norm, ffn1_gate, ffn1_up, ffn1_down)
    ffn2 = (ffn2_norm, ffn2_gate, ffn2_up, ffn2_down)

    near = _bias_tables(rel_bias)
    tri = _sb_tri(tk_sb)

    def ffn(xin, weights, layer):
        norm, wg, wu, wd = weights
        xn = _rmsnorm(xin, norm[layer])
        act = _matmul(xn, [wg, wu], layer, mode="swiglu", out_dtype=BF16, tm=1024, tn=256, name="ffn_up")
        return _matmul(act, [wd], layer, mode="residual", out_dtype=F32, tm=1024, tn=256, extra=xin,
                       res_scale=0.5, rows_outer=True, name="ffn_down")

    h = x[0]
    for i in range(depth):
        h = ffn(h, ffn1, i)

        hn = _rmsnorm(h, mix_norm[i])
        gain = jnp.concatenate([jnp.tile(q_norm[i] * DSA_LOGIT_SCALE2, dsa_heads),
                                jnp.tile(k_norm[i], dsa_heads)])[None, :]
        proj = functools.partial(_matmul, hn, [w_in_t], i, tm=1024, w_nk=True)
        u_pool = proj(mode="cast", out_dtype=F32, tn=512, col0=o[0], n=pool_width, name="proj_pool")
        qk = proj(mode="headnorm", out_dtype=BF16, tn=512, col0=o[1], n=2 * dsa_width, extra=gain,
                  name="proj_qk")
        v_qi = proj(mode="cast", out_dtype=BF16, tn=512, col0=o[3], n=o[5] - o[3], name="proj_v_qi")
        small = proj(mode="cast", out_dtype=F32, tn=LANES, col0=o[5], n=LANES, name="proj_small")
        qkv_sb = proj(mode="cast", out_dtype=BF16, tn=512, col0=o[7], n=3 * sb_width, name="proj_sb")

        y_pool = _pool_mixer(u_pool, pool_w[i], pool_scale[i])

        qi_hm = jnp.transpose(v_qi[:, dsa_width:].reshape(s, IDX_HEADS, IDX_DIM), (1, 0, 2))
        kit = jnp.transpose(small[:, :IDX_DIM].astype(BF16).reshape(s // t, t, IDX_DIM), (0, 2, 1))
        wi = small[:, IDX_DIM:IDX_DIM + IDX_HEADS]
        mask = _dsa_index(qi_hm, kit, wi, t=t, topk=topk)
        y_dsa = _dsa_attn(qk, v_qi, mask, near, t=t_att, heads=dsa_heads)

        y_sb = _sb_attn(qkv_sb, tri, t=t_sb, tk=tk_sb, heads=sb_heads, q_blk=0, k_blk=sb_heads,
                        v_blk=2 * sb_heads)

        h = _mix_out([y_pool, y_dsa, y_sb], w_out, i, h, tm=1024, tn=512)

        h = ffn(h, ffn2, i)
    return h[None]
```

```python
import functools
import math

import jax
import jax.numpy as jnp
import numpy as np
from jax import lax
from jax.experimental import pallas as pl
from jax.experimental.pallas import tpu as pltpu

HEAD_DIM = 128
POOL_WINDOWS = (2, 4, 8, 16)
IDX_HEADS = 16
IDX_DIM = 64
INDEX_TOPK = 256
N_BUCKETS = 32
MAX_DISTANCE = 128
EPS = 1e-6

LANES = 128
SUBLANES = 8
POOL_HISTORY = 16
NEG = -1e30
INT_MIN = -(2 ** 31)
F32_MIN_NORMAL = 2.0 ** -126
BF16_MIN_NORMAL_KEY = 0x0080
DSA_LOGIT_SCALE2 = HEAD_DIM ** -0.5 * math.log2(math.e)
SB_EXIT_BITS = 160.0
VMEM_LIMIT = 56 * 1024 * 1024

F32 = jnp.float32
BF16 = jnp.bfloat16


def _cparams(n_grid):
    return pltpu.CompilerParams(dimension_semantics=("arbitrary",) * n_grid,
                                vmem_limit_bytes=VMEM_LIMIT)


def _pick(n, pref):
    t = min(pref, n)
    while n % t:
        t //= 2
    return t


def _rmsnorm_kernel(x_ref, g_ref, o_ref):
    x = x_ref[...]
    ms = jnp.mean(x * x, axis=-1, keepdims=True)
    o_ref[...] = (x * lax.rsqrt(ms + EPS) * g_ref[...]).astype(o_ref.dtype)


def _rmsnorm(x, g):
    s, d = x.shape
    tm = _pick(s, 512)
    return pl.pallas_call(
        _rmsnorm_kernel,
        out_shape=jax.ShapeDtypeStruct((s, d), BF16),
        grid=(s // tm,),
        in_specs=[pl.BlockSpec((tm, d), lambda i: (i, 0)),
                  pl.BlockSpec((1, d), lambda i: (0, 0))],
        out_specs=pl.BlockSpec((tm, d), lambda i: (i, 0)),
        compiler_params=_cparams(1),
        name="rmsnorm",
    )(x, g.reshape(1, d))


def _mm_kernel(a_ref, *refs, n_w, mode, res_scale, w_nk, rows_outer):
    if rows_outer:
        w_refs, extra, o_ref = refs[:n_w], refs[n_w:-1], refs[-1]
        wb_refs = w_refs
    else:
        w_refs, extra, o_ref, wb_refs = refs[:n_w], refs[n_w:-1 - n_w], refs[-1 - n_w], refs[-n_w:]

        @pl.when(pl.program_id(1) == 0)
        def _():
            for w_ref, wb_ref in zip(w_refs, wb_refs):
                wb_ref[...] = w_ref[...].astype(BF16)

    a = a_ref[...]
    contract = (((1,), (1 if w_nk else 0,)), ((), ()))

    def dot(wb_ref):
        return lax.dot_general(a, wb_ref[...].astype(BF16), contract, preferred_element_type=F32)

    if mode == "swiglu":
        g = dot(wb_refs[0])
        u = dot(wb_refs[1])
        o_ref[...] = (g / (1.0 + jnp.exp(-g)) * u).astype(o_ref.dtype)
        return
    acc = dot(wb_refs[0])
    if mode == "cast":
        o_ref[...] = acc.astype(o_ref.dtype)
    elif mode == "residual":
        o_ref[...] = extra[0][...] + res_scale * acc
    elif mode == "headnorm":
        gain = extra[0][...]
        for c in range(acc.shape[1] // HEAD_DIM):
            sl = slice(c * HEAD_DIM, (c + 1) * HEAD_DIM)
            y = acc[:, sl]
            ms = jnp.mean(y * y, axis=-1, keepdims=True)
            o_ref[:, sl] = (y * lax.rsqrt(ms + EPS) * gain[:, sl]).astype(o_ref.dtype)
    else:
        raise ValueError(mode)


def _matmul(a, ws, layer, *, mode, out_dtype, tm, tn, col0=0, n=None, extra=None, res_scale=1.0, w_nk=False,
            rows_outer=False, name):
    m, k = a.shape
    n = ws[0].shape[1 if w_nk else 2] if n is None else n
    tm, tn = _pick(m, tm), _pick(n, tn)

    def at(f):
        return (lambda i, j: f(i, j)) if rows_outer else (lambda j, i: f(i, j))

    in_specs = [pl.BlockSpec((tm, k), at(lambda i, j: (i, 0)))]
    if w_nk:
        assert col0 % SUBLANES == 0
        n_all = ws[0].shape[1]
        assert n_all % SUBLANES == 0
        ws = [w.reshape(-1, k) for w in ws]
        in_specs += [pl.BlockSpec((pl.Element(tn), pl.Element(k)),
                                  at(lambda i, j: (pl.multiple_of(layer * n_all + col0 + j * tn, SUBLANES), 0)))
                     for _ in ws]
    else:
        assert col0 % tn == 0
        in_specs += [pl.BlockSpec((None, k, tn), at(lambda i, j: (layer, 0, col0 // tn + j))) for _ in ws]
    operands = [a] + list(ws)
    if mode == "residual":
        in_specs.append(pl.BlockSpec((tm, tn), at(lambda i, j: (i, j))))
        operands.append(extra)
    elif mode == "headnorm":
        in_specs.append(pl.BlockSpec((1, tn), at(lambda i, j: (0, j))))
        operands.append(extra)
    scratch = [] if rows_outer else [pltpu.VMEM((tn, k) if w_nk else (k, tn), BF16) for _ in ws]
    return pl.pallas_call(
        functools.partial(_mm_kernel, n_w=len(ws), mode=mode, res_scale=res_scale, w_nk=w_nk,
                          rows_outer=rows_outer),
        out_shape=jax.ShapeDtypeStruct((m, n), out_dtype),
        grid=(m // tm, n // tn) if rows_outer else (n // tn, m // tm),
        in_specs=in_specs,
        out_specs=pl.BlockSpec((tm, tn), at(lambda i, j: (i, j))),
        scratch_shapes=scratch,
        compiler_params=_cparams(2),
        name=name,
    )(*operands)


def _mix_out_kernel(*refs, n_in):
    y_refs, w_refs = refs[:n_in], refs[n_in:2 * n_in]
    res_ref, o_ref, wb_refs = refs[2 * n_in], refs[2 * n_in + 1], refs[2 * n_in + 2:]

    @pl.when(pl.program_id(1) == 0)
    def _():
        for w_ref, wb_ref in zip(w_refs, wb_refs):
            wb_ref[...] = w_ref[...].astype(BF16)

    acc = res_ref[...]
    for y_ref, wb_ref in zip(y_refs, wb_refs):
        acc = acc + jnp.dot(y_ref[...], wb_ref[...], preferred_element_type=F32)
    o_ref[...] = acc


def _mix_out(ys, w_out, layer, res, *, tm, tn):
    m, n = res.shape
    k = w_out.shape[1]
    assert sum(y.shape[1] for y in ys) == k
    tm, tn = _pick(m, tm), _pick(n, tn)
    w2 = w_out.reshape(-1, n)
    row0 = [int(r) for r in np.cumsum([0] + [y.shape[1] for y in ys[:-1]])]
    assert all(r % SUBLANES == 0 for r in row0) and k % SUBLANES == 0

    def w_spec(kp, r):
        return pl.BlockSpec((pl.Element(kp), pl.Element(tn)),
                            lambda j, i: (pl.multiple_of(layer * k + r, SUBLANES), pl.multiple_of(j * tn, LANES)))

    return pl.pallas_call(
        functools.partial(_mix_out_kernel, n_in=len(ys)),
        out_shape=jax.ShapeDtypeStruct((m, n), F32),
        grid=(n // tn, m // tm),
        in_specs=([pl.BlockSpec((tm, y.shape[1]), lambda j, i: (i, 0)) for y in ys]
                  + [w_spec(y.shape[1], r) for y, r in zip(ys, row0)]
                  + [pl.BlockSpec((tm, tn), lambda j, i: (i, j))]),
        out_specs=pl.BlockSpec((tm, tn), lambda j, i: (i, j)),
        scratch_shapes=[pltpu.VMEM((y.shape[1], tn), BF16) for y in ys],
        compiler_params=_cparams(2),
        name="mix_out",
    )(*ys, *([w2] * len(ys)), res)


def _pool_kernel(cur_ref, prev_ref, w_ref, scale_ref, o_ref, *, tm, gd):
    i = pl.program_id(0)
    cur = cur_ref[...]
    prev = jnp.where(i > 0, prev_ref[...], 0.0)
    ext = jnp.concatenate([prev, cur], axis=0)
    t = i * tm + lax.broadcasted_iota(jnp.int32, (tm, 1), 0)
    sums = {1: ext}
    w = 1
    while w < max(POOL_WINDOWS):
        sums[2 * w] = sums[w] + pltpu.roll(sums[w], w, axis=0)
        w *= 2
    for g, win in enumerate(POOL_WINDOWS):
        sl = slice(g * gd, (g + 1) * gd)
        cnt = jnp.minimum(t + 1, win).astype(F32)
        d = sums[win][POOL_HISTORY:, sl] / cnt - cur[:, sl]
        y = jnp.dot(d.astype(BF16), w_ref[g], preferred_element_type=F32)
        o_ref[:, sl] = (y * scale_ref[:, sl]).astype(o_ref.dtype)


def _pool_mixer(u_pool, pool_w, pool_scale):
    s, pw = u_pool.shape
    groups, gd, _ = pool_w.shape
    tm = _pick(s, 512)
    hist_blocks = tm // POOL_HISTORY
    return pl.pallas_call(
        functools.partial(_pool_kernel, tm=tm, gd=gd),
        out_shape=jax.ShapeDtypeStruct((s, pw), BF16),
        grid=(s // tm,),
        in_specs=[pl.BlockSpec((tm, pw), lambda i: (i, 0)),
                  pl.BlockSpec((POOL_HISTORY, pw), lambda i: (jnp.maximum(i * hist_blocks - 1, 0), 0)),
                  pl.BlockSpec((groups, gd, gd), lambda i: (0, 0, 0)),
                  pl.BlockSpec((1, pw), lambda i: (0, 0))],
        out_specs=pl.BlockSpec((tm, pw), lambda i: (i, 0)),
        compiler_params=_cparams(1),
        name="pool_mixer",
    )(u_pool, u_pool, pool_w.astype(BF16), pool_scale.reshape(1, pw))


def _dsa_index_kernel(qi_ref, kit_ref, w_ref, o_ref, keys_ref, keyst_ref, hit_ref, wb_ref, *, t, topk):
    qb = pl.program_id(0)
    nkb = o_ref.shape[0]
    w = w_ref[...] * (IDX_DIM ** -0.5 * IDX_HEADS ** -0.5)
    for h in range(IDX_HEADS):
        wb_ref[h] = jnp.broadcast_to(w[:, h:h + 1], (t, t))
    row = qb * t + lax.broadcasted_iota(jnp.int32, (t, t), 0)
    col0 = lax.broadcasted_iota(jnp.int32, (t, t), 1)

    def scores(kb):
        kt = kit_ref[kb]
        acc = jnp.zeros((t, t), F32)
        for h in range(IDX_HEADS):
            sc = jnp.dot(qi_ref[h], kt, preferred_element_type=F32)
            acc = acc + jnp.maximum(sc, 0.0) * wb_ref[h]
        return acc

    def store_keys(kb, acc):
        acc = jnp.where(jnp.abs(acc) < F32_MIN_NORMAL, 0.0, acc)
        bits = pltpu.bitcast(acc, jnp.int32)
        key = jnp.where(bits < 0, bits ^ 0x7FFFFFFF, bits)
        key = jnp.where(kb * t + col0 <= row, key, INT_MIN)
        keys_ref[kb] = key
        key_t = key.T
        keyst_ref[kb] = key_t
        bits_t = jnp.where(key_t < 0, key_t ^ 0x7FFFFFFF, key_t) & -65536
        hit_ref[kb] = pltpu.bitcast(bits_t, F32).astype(BF16)

    def score_pair(i, carry):
        acc0, acc1 = scores(2 * i), scores(2 * i + 1)
        store_keys(2 * i, acc0)
        store_keys(2 * i + 1, acc1)
        return carry

    def score_tile(kb, carry):
        store_keys(kb, scores(kb))
        return carry

    lax.fori_loop(0, (qb + 1) // 2, score_pair, 0)
    lax.fori_loop(2 * ((qb + 1) // 2), qb + 1, score_tile, 0)

    n_acc = 4
    pack = 2 * SUBLANES

    def hi_step(it, state):
        thr16, kept = state
        cand16 = thr16 + jnp.left_shift(jnp.int32(1), 15 - it)
        c16 = jnp.where(jnp.logical_and(cand16 > 0, cand16 < BF16_MIN_NORMAL_KEY), BF16_MIN_NORMAL_KEY, cand16)
        c16 = jnp.where(jnp.logical_and(c16 < 0, c16 >= -BF16_MIN_NORMAL_KEY), 0, c16)
        cand_bits = jnp.left_shift(jnp.where(c16 < 0, c16 ^ 0x7FFF, c16), 16)
        cand_b = jnp.broadcast_to(pltpu.bitcast(cand_bits, F32).astype(BF16), (pack, t))

        def count_tile(kb, cs):
            cs = list(cs)
            for g in range(t // pack):
                xh = hit_ref[kb, g * pack:(g + 1) * pack, :]
                cs[g % n_acc] = cs[g % n_acc] + jnp.where(xh >= cand_b, jnp.ones((), BF16), jnp.zeros((), BF16))
            return tuple(cs)

        cs = lax.fori_loop(0, qb + 1, count_tile, tuple(jnp.zeros((pack, t), BF16) for _ in range(n_acc)))
        c = (cs[0].astype(F32) + cs[1].astype(F32)) + (cs[2].astype(F32) + cs[3].astype(F32))
        cnt = jnp.sum(c, axis=0, keepdims=True)
        ok = cnt >= float(topk)
        return jnp.where(ok, cand16, thr16), jnp.where(ok, cnt, kept)

    def lo_step(it, state):
        thr, kept = state
        cand = thr + jnp.left_shift(jnp.int32(1), 15 - it)
        cand_b = jnp.broadcast_to(cand, (SUBLANES, t))

        def count_tile(kb, cs):
            cs = list(cs)
            for g in range(t // SUBLANES):
                kk = keyst_ref[kb, g * SUBLANES:(g + 1) * SUBLANES, :]
                cs[g % n_acc] = cs[g % n_acc] + jnp.where(kk >= cand_b, 1, 0)
            return tuple(cs)

        cs = lax.fori_loop(0, qb + 1, count_tile,
                           tuple(jnp.zeros((SUBLANES, t), jnp.int32) for _ in range(n_acc)))
        c = (cs[0] + cs[1]) + (cs[2] + cs[3])
        cnt = jnp.sum(c.astype(F32), axis=0, keepdims=True)
        ok = cnt >= float(topk)
        return jnp.where(ok, cand, thr), jnp.where(ok, cnt, kept)

    assert nkb * t <= 256 * pack * n_acc
    thr16, kept = lax.fori_loop(0, 16, hi_step,
                                (jnp.full((1, t), -(2 ** 15), jnp.int32), jnp.zeros((1, t), F32)))
    thr, kept = lax.fori_loop(0, 16, lo_step, (jnp.left_shift(thr16, 16), kept))
    thr = jnp.maximum(thr, INT_MIN + 1)

    def count_keys(pred):
        def count_tile(kb, cs):
            cs = list(cs)
            for g in range(t // SUBLANES):
                kk = keyst_ref[kb, g * SUBLANES:(g + 1) * SUBLANES, :]
                idx = kb * t + g * SUBLANES + lax.broadcasted_iota(jnp.int32, (SUBLANES, t), 0)
                cs[g % n_acc] = cs[g % n_acc] + jnp.where(pred(kk, idx), 1, 0)
            return tuple(cs)

        cs = lax.fori_loop(0, qb + 1, count_tile,
                           tuple(jnp.zeros((SUBLANES, t), jnp.int32) for _ in range(n_acc)))
        return jnp.sum(((cs[0] + cs[1]) + (cs[2] + cs[3])).astype(F32), axis=0, keepdims=True)

    def break_ties():
        thr_s = jnp.broadcast_to(thr, (SUBLANES, t))
        need = float(topk) - count_keys(lambda kk, idx: kk > thr_s)

        def idx_step(it, last):
            cand = last + jnp.left_shift(jnp.int32(1), idx_bits - 1 - it)
            cand_s = jnp.broadcast_to(cand, (SUBLANES, t))
            below = count_keys(lambda kk, idx: jnp.logical_and(kk == thr_s, idx < cand_s))
            return jnp.where(below < need, cand, last)

        return lax.fori_loop(0, idx_bits, idx_step, jnp.zeros((1, t), jnp.int32))

    idx_bits = max(1, (nkb * t - 1).bit_length())
    tied = jnp.max(jnp.where(kept > float(topk), 1.0, 0.0)) > 0.0
    last = lax.cond(tied, break_ties, lambda: jnp.full((1, t), nkb * t, jnp.int32))

    def per_row(v):
        v = jnp.broadcast_to(v, (LANES, t)).T
        return jnp.concatenate([v] * (t // LANES), axis=1)

    thr_b, last_b = per_row(thr), per_row(last)

    def write_tile(kb, carry):
        o_ref[kb] = jnp.where(keys_ref[kb] >= thr_b, 0.0, NEG).astype(o_ref.dtype)
        return carry

    def write_tile_tied(kb, carry):
        key = keys_ref[kb]
        keep = jnp.logical_or(key > thr_b, jnp.logical_and(key == thr_b, kb * t + col0 <= last_b))
        o_ref[kb] = jnp.where(keep, 0.0, NEG).astype(o_ref.dtype)
        return carry

    def fill_tile(kb, carry):
        o_ref[kb] = jnp.full((t, t), NEG, o_ref.dtype)
        return carry

    n_tied = jnp.where(tied, qb + 1, 0)
    lax.fori_loop(0, n_tied, write_tile_tied, 0)
    lax.fori_loop(n_tied, qb + 1, write_tile, 0)
    lax.fori_loop(qb + 1, nkb, fill_tile, 0)


def _dsa_index(qi_hm, kit, wi, *, t, topk):
    heads, s, idim = qi_hm.shape
    nkb = s // t
    return pl.pallas_call(
        functools.partial(_dsa_index_kernel, t=t, topk=topk),
        out_shape=jax.ShapeDtypeStruct((nkb, s, t), BF16),
        grid=(nkb,),
        in_specs=[pl.BlockSpec((heads, t, idim), lambda i: (0, i, 0)),
                  pl.BlockSpec((nkb, idim, t), lambda i: (0, 0, 0)),
                  pl.BlockSpec((t, heads), lambda i: (i, 0))],
        out_specs=pl.BlockSpec((nkb, t, t), lambda i: (0, i, 0)),
        scratch_shapes=[pltpu.VMEM((nkb, t, t), jnp.int32),
                        pltpu.VMEM((nkb, t, t), jnp.int32),
                        pltpu.VMEM((nkb, t, t), BF16),
                        pltpu.VMEM((heads, t, t), F32)],
        compiler_params=_cparams(1),
        name="dsa_index",
    )(qi_hm, kit, wi)


def _dsa_attn_kernel(q_ref, k_ref, v_ref, mask_ref, near_ref, o_ref, s0_ref, s1_ref, *, t, tm):
    qb = pl.program_id(1)
    q = q_ref[...]
    per = t // tm
    nb = t // MAX_DISTANCE

    def add_near(s, which):
        out_rows = []
        for i in range(nb):
            blocks = []
            for j in range(nb):
                blk = s[i * MAX_DISTANCE:(i + 1) * MAX_DISTANCE, j * MAX_DISTANCE:(j + 1) * MAX_DISTANCE]
                if which == "diag" and i == j:
                    blk = blk + near_ref[0, 0]
                elif (which == "diag" and i == j + 1) or (which == "before" and i == 0 and j == nb - 1):
                    blk = blk + near_ref[0, 1]
                blocks.append(blk)
            out_rows.append(jnp.concatenate(blocks, axis=1))
        return jnp.concatenate(out_rows, axis=0)

    bufs = (s0_ref, s1_ref)

    def logits_into(kc, dst_ref):
        rows = pl.ds(pl.multiple_of(kc * t, t), t)
        dst_ref[...] = lax.dot_general(q, k_ref[rows, :], (((1,), (1,)), ((), ())), preferred_element_type=F32)

    def step(kc, which, parity, carry):
        if which != "diag":
            logits_into(kc + 1, bufs[1 - parity])
        m, acc = carry
        s = bufs[parity][...] + jnp.concatenate([mask_ref[kc * per + j] for j in range(per)],
                                                axis=1).astype(F32)
        if which != "far":
            s = add_near(s, which)
        m_new = jnp.maximum(m, jnp.max(s, axis=1, keepdims=True))
        p = jnp.exp2(s - m_new)
        rows = pl.ds(pl.multiple_of(kc * t, t), t)
        v = v_ref[rows, :]
        v_ones = jnp.concatenate([v, jnp.ones_like(v)], axis=1)
        acc = jnp.exp2(m - m_new) * acc + jnp.dot(p.astype(BF16), v_ones, preferred_element_type=F32)
        return m_new, acc

    def far_pair(i, carry):
        carry = step(2 * i, "far", 0, carry)
        return step(2 * i + 1, "far", 1, carry)

    def tail(kinds):
        def run(k0, carry):
            for n, which in enumerate(kinds):
                carry = step(k0 + n, which, n % 2, carry)
            return carry
        return run

    logits_into(0, s0_ref)
    n_far = jnp.maximum(qb - 1, 0)
    carry = (jnp.full((t, 1), NEG, F32), jnp.zeros((t, 2 * HEAD_DIM), F32))
    carry = lax.fori_loop(0, n_far // 2, far_pair, carry)
    k0 = 2 * (n_far // 2)
    case = jnp.where(qb == 0, 0, 1 + n_far % 2)
    _, acc = lax.switch(case, [tail(("diag",)), tail(("before", "diag")), tail(("far", "before", "diag"))],
                        k0, carry)
    o_ref[...] = (acc[:, :HEAD_DIM] / acc[:, HEAD_DIM:]).astype(o_ref.dtype)


def _dsa_attn(qk, v, mask, near, *, t, heads):
    s = qk.shape[0]
    nkb, _, tm = mask.shape
    return pl.pallas_call(
        functools.partial(_dsa_attn_kernel, t=t, tm=tm),
        out_shape=jax.ShapeDtypeStruct((s, heads * HEAD_DIM), BF16),
        grid=(heads, s // t),
        in_specs=[pl.BlockSpec((t, HEAD_DIM), lambda h, i: (i, h)),
                  pl.BlockSpec((s, HEAD_DIM), lambda h, i: (0, heads + h)),
                  pl.BlockSpec((s, HEAD_DIM), lambda h, i: (0, h)),
                  pl.BlockSpec((nkb, t, tm), lambda h, i: (0, i, 0)),
                  pl.BlockSpec((1, 2, MAX_DISTANCE, MAX_DISTANCE), lambda h, i: (h, 0, 0, 0))],
        out_specs=pl.BlockSpec((t, HEAD_DIM), lambda h, i: (i, h)),
        scratch_shapes=[pltpu.VMEM((t, t), F32), pltpu.VMEM((t, t), F32)],
        compiler_params=_cparams(2),
        name="dsa_attn",
    )(qk, qk, v, mask, near)


def _sb_kernel(q_ref, k_ref, v_ref, tri_ref, o_ref, *, t, tk):
    qb = pl.program_id(1)
    q = q_ref[...]
    tri = tri_ref[...]
    scale2 = HEAD_DIM ** -0.5 * math.log2(math.e)
    per = t // tk
    small = max(per // 2, 1)

    def chunk(kb0, n, masked, carry):
        before, acc = carry
        width = n * tk
        rows = pl.ds(pl.multiple_of(kb0 * tk, tk), width)
        z = lax.dot_general(q, k_ref[rows, :], (((1,), (1,)), ((), ())), preferred_element_type=F32) * scale2
        lo_z, hi_z = jnp.minimum(z, 0.0), jnp.maximum(z, 0.0)
        lg = jnp.log2(1.0 + jnp.exp2(lo_z - hi_z))
        sp = hi_z + lg
        log_beta = lo_z - lg
        if masked:
            strict = kb0 * tk + lax.broadcasted_iota(jnp.int32, (t, width), 1) < row
            sp = jnp.where(strict, sp, 0.0)
        hi = sp.astype(BF16)
        lo = (sp - hi.astype(F32)).astype(BF16)
        cs = []
        for u in range(n):
            sl = slice(u * tk, (u + 1) * tk)
            cs.append(jnp.dot(jnp.concatenate([hi[:, sl], lo[:, sl]], axis=1), tri, preferred_element_type=F32))
        a = [None] * n
        for u in reversed(range(n)):
            a[u] = jnp.exp2(log_beta[:, u * tk:(u + 1) * tk] - (before + cs[u][:, :tk]))
            before = before + cs[u][:, tk:]
        a = jnp.concatenate(a, axis=1)
        if masked:
            a = jnp.where(strict, a, 0.0)
        acc = acc + jnp.dot(a.astype(BF16), v_ref[rows, :], preferred_element_type=F32)
        return before, acc

    def walk(n_iters, kb0_of, n, state):
        def cond(c):
            return jnp.logical_and(c[0] < n_iters, c[1])

        def body(c):
            before, acc = chunk(kb0_of(c[0]), n, False, c[2:])
            return c[0] + 1, jnp.min(before) < SB_EXIT_BITS, before, acc

        return lax.while_loop(cond, body, (jnp.int32(0),) + state[1:])

    row = qb * t + lax.broadcasted_iota(jnp.int32, (t, t), 0)
    carry = (jnp.zeros((t, tk), F32), jnp.zeros((t, HEAD_DIM), F32))
    before, acc = chunk(qb * per, per, True, carry)
    state = (jnp.int32(0), jnp.min(before) < SB_EXIT_BITS, before, acc)
    n_rest = qb * per
    n_small = jnp.minimum(n_rest, per) // small
    state = walk(n_small, lambda i: n_rest - (i + 1) * small, small, state)
    state = walk(qb - 1, lambda i: n_rest - per - (i + 1) * per, per, state)
    o_ref[...] = state[3].astype(o_ref.dtype)


def _sb_tri(tk):
    j = jnp.arange(tk)
    half = jnp.concatenate([j[:, None] > j[None, :], jnp.ones((tk, tk), bool)], axis=1)
    return jnp.concatenate([half, half], axis=0).astype(BF16)


def _sb_attn(rest, tri, *, t, tk, heads, q_blk, k_blk, v_blk):
    s = rest.shape[0]
    return pl.pallas_call(
        functools.partial(_sb_kernel, t=t, tk=tk),
        out_shape=jax.ShapeDtypeStruct((s, heads * HEAD_DIM), BF16),
        grid=(heads, s // t),
        in_specs=[pl.BlockSpec((t, HEAD_DIM), lambda h, i: (i, q_blk + h)),
                  pl.BlockSpec((s, HEAD_DIM), lambda h, i: (0, k_blk + h)),
                  pl.BlockSpec((s, HEAD_DIM), lambda h, i: (0, v_blk + h)),
                  pl.BlockSpec((2 * tk, 2 * tk), lambda h, i: (0, 0))],
        out_specs=pl.BlockSpec((t, HEAD_DIM), lambda h, i: (i, h)),
        compiler_params=_cparams(2),
        name="sb_attn",
    )(rest, rest, rest, tri)


def _rel_bucket(n):
    max_exact = N_BUCKETS // 2
    nf = jnp.maximum(n, 1).astype(F32)
    large = max_exact + (jnp.log(nf / max_exact) / math.log(MAX_DISTANCE / max_exact)
                         * (N_BUCKETS - max_exact)).astype(jnp.int32)
    large = jnp.minimum(large, N_BUCKETS - 1)
    return jnp.where(n < max_exact, n, large)


def _bias_tables(rel_bias):
    b = MAX_DISTANCE
    by_dist = (rel_bias[_rel_bucket(jnp.arange(2 * b))] - rel_bias[N_BUCKETS - 1]) * math.log2(math.e)
    r = jnp.arange(b)[:, None]
    c = jnp.arange(b)[None, :]
    near = jnp.stack([by_dist[jnp.maximum(r - c, 0)], by_dist[b + r - c]])
    return jnp.transpose(near, (3, 0, 1, 2))


def kernel(x, positions, rel_bias, ffn1_norm, ffn1_gate, ffn1_up, ffn1_down, mix_norm, w_in, pool_w,
           pool_scale, q_norm, k_norm, w_out, ffn2_norm, ffn2_gate, ffn2_up, ffn2_down):
    del positions
    b, s, d = x.shape
    assert b == 1
    depth = w_in.shape[0]
    pool_width = pool_w.shape[1] * pool_w.shape[2]
    dsa_width = (w_in.shape[2] - pool_width - IDX_HEADS * IDX_DIM - IDX_DIM - IDX_HEADS) // 6
    sb_width = dsa_width
    dsa_heads = dsa_width // HEAD_DIM
    sb_heads = sb_width // HEAD_DIM
    topk = min(INDEX_TOPK, s // 4)
    t = _pick(s, 256)
    t_att = _pick(s, 512)
    tk_sb = 128
    t_sb = t_att

    o = [int(c) for c in np.cumsum([0, pool_width, dsa_width, dsa_width, dsa_width, IDX_HEADS * IDX_DIM, IDX_DIM,
                                    IDX_HEADS, sb_width, sb_width, sb_width])]
    w_in_t = jnp.swapaxes(w_in, 1, 2)
    ffn1 = (ffn1_norm, ffn1_gate, ffn1_up, ffn1_down)
    ffn2 = (ffn2_norm, ffn2_gate, ffn2_up, ffn2_down)

    near = _bias_tables(rel_bias)
    tri = _sb_tri(tk_sb)

    def ffn(xin, weights, layer):
        norm, wg, wu, wd = weights
        xn = _rmsnorm(xin, norm[layer])
        act = _matmul(xn, [wg, wu], layer, mode="swiglu", out_dtype=BF16, tm=1024, tn=256, name="ffn_up")
        return _matmul(act, [wd], layer, mode="residual", out_dtype=F32, tm=1024, tn=256, extra=xin,
                       res_scale=0.5, rows_outer=True, name="ffn_down")

    h = x[0]
    for i in range(depth):
        h = ffn(h, ffn1, i)

        hn = _rmsnorm(h, mix_norm[i])
        gain = jnp.concatenate([jnp.tile(q_norm[i] * DSA_LOGIT_SCALE2, dsa_heads),
                                jnp.tile(k_norm[i], dsa_heads)])[None, :]
        proj = functools.partial(_matmul, hn, [w_in_t], i, tm=1024, w_nk=True)
        u_pool = proj(mode="cast", out_dtype=F32, tn=512, col0=o[0], n=pool_width, name="proj_pool")
        qk = proj(mode="headnorm", out_dtype=BF16, tn=512, col0=o[1], n=2 * dsa_width, extra=gain,
                  name="proj_qk")
        v_qi = proj(mode="cast", out_dtype=BF16, tn=512, col0=o[3], n=o[5] - o[3], name="proj_v_qi")
        small = proj(mode="cast", out_dtype=F32, tn=LANES, col0=o[5], n=LANES, name="proj_small")
        qkv_sb = proj(mode="cast", out_dtype=BF16, tn=512, col0=o[7], n=3 * sb_width, name="proj_sb")

        y_pool = _pool_mixer(u_pool, pool_w[i], pool_scale[i])

        qi_hm = jnp.transpose(v_qi[:, dsa_width:].reshape(s, IDX_HEADS, IDX_DIM), (1, 0, 2))
        kit = jnp.transpose(small[:, :IDX_DIM].astype(BF16).reshape(s // t, t, IDX_DIM), (0, 2, 1))
        wi = small[:, IDX_DIM:IDX_DIM + IDX_HEADS]
        mask = _dsa_index(qi_hm, kit, wi, t=t, topk=topk)
        y_dsa = _dsa_attn(qk, v_qi, mask, near, t=t_att, heads=dsa_heads)

        y_sb = _sb_attn(qkv_sb, tri, t=t_sb, tk=tk_sb, heads=sb_heads, q_blk=0, k_blk=sb_heads,
                        v_blk=2 * sb_heads)

        h = _mix_out([y_pool, y_dsa, y_sb], w_out, i, h, tm=1024, tn=512)

        h = ffn(h, ffn2, i)
    return h[None]
```
